```python
import math
import jax, jax.numpy as jnp
from jax import lax
import numpy as np

D_MODEL = 1024
BATCH = 8
SEQ = 4096
DEPTH = 1

CHUNK = 64
N_META = 16
DC = D_MODEL
CONF_K = 31
DN_HEADS = 8
DN_DK = 128
DN_DV = 128
DN_CONV_K = 4
D_FF = 2816
FFN_CONV_K = 3
N_BRANCH = 2
EPS = 1e-6
HK = DN_HEADS * DN_DK
HV = DN_HEADS * DN_DV
D_IN = 2 * DC + 2 * HK + 2 * HV + 2 * DN_HEADS + N_BRANCH * D_MODEL

kernel_name = 'hybrid_conformer_gdn_convffn_block'


def rmsnorm(x, w):
    xf = x.astype(jnp.float32)
    y = xf * lax.rsqrt(jnp.mean(xf * xf, axis=-1, keepdims=True) + EPS)
    return (y * w.astype(jnp.float32)).astype(x.dtype)


def layernorm(x, w, b):
    xf = x.astype(jnp.float32)
    mu = jnp.mean(xf, axis=-1, keepdims=True)
    xc = xf - mu
    y = xc * lax.rsqrt(jnp.mean(xc * xc, axis=-1, keepdims=True) + EPS)
    return (y * w.astype(jnp.float32) + b.astype(jnp.float32)).astype(x.dtype)


def l2norm(x):
    xf = x.astype(jnp.float32)
    return xf * lax.rsqrt(jnp.sum(xf * xf, axis=-1, keepdims=True) + EPS)


def causal_dwconv(x, w, b=None):
    k = w.shape[0]
    y = lax.conv_general_dilated(
        x, w[:, None, :].astype(x.dtype), window_strides=(1,), padding=[(k - 1, 0)],
        dimension_numbers=('NWC', 'WIO', 'NWC'), feature_group_count=x.shape[-1])
    return y if b is None else y + b.astype(x.dtype)


def chunk_gated_delta_rule(q, k, v, g, beta):
    b, lp, h, dk = q.shape
    dv = v.shape[-1]
    n = lp // CHUNK

    def to_chunks(t):
        t = t.astype(jnp.float32).reshape((b, n, CHUNK, h) + t.shape[3:])
        return jnp.moveaxis(t, (1, 3), (0, 2))

    qc = to_chunks(q) * (dk ** -0.5)
    kc = to_chunks(k)
    vc = to_chunks(v)
    bc = to_chunks(beta)
    gc = jnp.cumsum(to_chunks(g), axis=-1)
    idx = jnp.arange(CHUNK)
    incl = idx[:, None] >= idx[None, :]
    strict = idx[:, None] > idx[None, :]
    decay = jnp.exp(jnp.where(incl, gc[..., :, None] - gc[..., None, :], -jnp.inf))
    kb = kc * bc[..., None]
    a_kk = jnp.where(strict, jnp.einsum('nbhid,nbhjd->nbhij', kb, kc) * decay, 0.0)
    rhs = jnp.concatenate([vc * bc[..., None], kb * jnp.exp(gc)[..., None]], axis=-1)
    sol = lax.linalg.triangular_solve(a_kk, rhs, left_side=True, lower=True, unit_diagonal=True)
    u, w = sol[..., :dv], sol[..., dv:]
    a_qk = jnp.where(incl, jnp.einsum('nbhid,nbhjd->nbhij', qc, kc) * decay, 0.0)
    q_dec = qc * jnp.exp(gc)[..., None]
    g_last = gc[..., -1:]
    k_dec = kc * jnp.exp(g_last - gc)[..., None]
    chunk_decay = jnp.exp(g_last)[..., None]

    def step(state, xs):
        u_c, w_c, qd_c, aqk_c, kd_c, cd_c = xs
        v_new = u_c - jnp.einsum('bhck,bhkv->bhcv', w_c, state)
        o_c = (jnp.einsum('bhck,bhkv->bhcv', qd_c, state)
               + jnp.einsum('bhij,bhjv->bhiv', aqk_c, v_new))
        state = state * cd_c + jnp.einsum('bhck,bhcv->bhkv', kd_c, v_new)
        return state, o_c

    s0 = jnp.zeros((b, h, dk, dv), jnp.float32)
    _, o = lax.scan(step, s0, (u, w, q_dec, a_qk, k_dec, chunk_decay))
    return jnp.moveaxis(o, (0, 2), (1, 3)).reshape(b, lp, h, dv)


def hybrid_layer(h, norm_mix_w, w_in, b_gate, conf_dw_w, conf_dw_b, conf_ln_w, conf_ln_b,
                 w_conf_out, dn_conv_w, dn_A_log, dn_dt_bias, dn_norm_w, w_dn_out, w_out,
                 norm_ffn_w, w_up, ffn_dw_w, ffn_dw_b, w_down):
    b, l, _ = h.shape
    u = rmsnorm(h, norm_mix_w)
    proj = u @ w_in
    cuts = [2 * DC, 2 * DC + HK, 2 * DC + 2 * HK, 2 * DC + 2 * HK + HV,
            2 * DC + 2 * HK + 2 * HV, 2 * DC + 2 * HK + 2 * HV + DN_HEADS,
            2 * DC + 2 * HK + 2 * HV + 2 * DN_HEADS]
    c_in, q, k, v, z, a_dt, b_beta, gate_logits = jnp.split(proj, cuts, axis=-1)

    c_val, c_gate = jnp.split(c_in, 2, axis=-1)
    c = c_val * jax.nn.sigmoid(c_gate)
    c = causal_dwconv(c, conf_dw_w, conf_dw_b)
    c = jax.nn.silu(layernorm(c, conf_ln_w, conf_ln_b))
    y_conf = c @ w_conf_out

    qkv = jax.nn.silu(causal_dwconv(jnp.concatenate([q, k, v], axis=-1), dn_conv_w))
    q, k, v = jnp.split(qkv, [HK, 2 * HK], axis=-1)
    q = l2norm(q.reshape(b, l, DN_HEADS, DN_DK))
    k = l2norm(k.reshape(b, l, DN_HEADS, DN_DK))
    v = v.reshape(b, l, DN_HEADS, DN_DV)
    g = -jnp.exp(dn_A_log.astype(jnp.float32)) * jax.nn.softplus(
        a_dt.astype(jnp.float32) + dn_dt_bias.astype(jnp.float32))
    beta = jax.nn.sigmoid(b_beta.astype(jnp.float32))
    pad = CHUNK - N_META

    def lpad(t):
        return jnp.pad(t, ((0, 0), (pad, 0)) + ((0, 0),) * (t.ndim - 2))

    o = chunk_gated_delta_rule(lpad(q), lpad(k), lpad(v), lpad(g), lpad(beta))[:, pad:]
    o = rmsnorm(o, dn_norm_w).astype(h.dtype)
    o = o * jax.nn.silu(z.reshape(b, l, DN_HEADS, DN_DV))
    y_dn = o.reshape(b, l, HV) @ w_dn_out

    gates = jax.nn.sigmoid(gate_logits + b_gate)
    g_conf, g_dn = jnp.split(gates, N_BRANCH, axis=-1)
    h = h + (g_conf * y_conf + g_dn * y_dn) @ w_out

    u2 = rmsnorm(h, norm_ffn_w)
    up = causal_dwconv(u2 @ w_up, ffn_dw_w, ffn_dw_b)
    f_gate, f_val = jnp.split(up, 2, axis=-1)
    return h + (jax.nn.silu(f_gate) * f_val) @ w_down


def setup_inputs(seed: int = 0) -> dict:
    key = jax.random.key(seed)
    ks = jax.random.split(key, 24)
    f32 = jnp.float32

    def nrm(k, shape, scale):
        return scale * jax.random.normal(k, shape, f32)

    dt = jnp.exp(jax.random.uniform(ks[11], (DEPTH, DN_HEADS), f32, math.log(1e-3), math.log(1e-1)))
    return {
        'x': nrm(ks[0], (BATCH, SEQ, D_MODEL), 1.0),
        'meta_tokens': nrm(ks[1], (N_META, D_MODEL), 1.0),
        'norm_mix_w': 1.0 + nrm(ks[2], (DEPTH, D_MODEL), 0.02),
        'w_in': nrm(ks[3], (DEPTH, D_MODEL, D_IN), D_MODEL ** -0.5),
        'b_gate': nrm(ks[4], (DEPTH, N_BRANCH * D_MODEL), 0.1),
        'conf_dw_w': nrm(ks[5], (DEPTH, CONF_K, DC), CONF_K ** -0.5),
        'conf_dw_b': nrm(ks[6], (DEPTH, DC), 0.02),
        'conf_ln_w': 1.0 + nrm(ks[7], (DEPTH, DC), 0.02),
        'conf_ln_b': nrm(ks[8], (DEPTH, DC), 0.02),
        'w_conf_out': nrm(ks[9], (DEPTH, DC, D_MODEL), DC ** -0.5),
        'dn_conv_w': nrm(ks[10], (DEPTH, DN_CONV_K, 2 * HK + HV), DN_CONV_K ** -0.5),
        'dn_A_log': jnp.log(jax.random.uniform(ks[12], (DEPTH, DN_HEADS), f32, 1.0, 16.0)),
        'dn_dt_bias': dt + jnp.log(-jnp.expm1(-dt)),
        'dn_norm_w': 1.0 + nrm(ks[13], (DEPTH, DN_DV), 0.02),
        'w_dn_out': nrm(ks[14], (DEPTH, HV, D_MODEL), HV ** -0.5),
        'w_out': nrm(ks[15], (DEPTH, D_MODEL, D_MODEL), D_MODEL ** -0.5),
        'norm_ffn_w': 1.0 + nrm(ks[16], (DEPTH, D_MODEL), 0.02),
        'w_up': nrm(ks[17], (DEPTH, D_MODEL, 2 * D_FF), D_MODEL ** -0.5),
        'ffn_dw_w': nrm(ks[18], (DEPTH, FFN_CONV_K, 2 * D_FF), FFN_CONV_K ** -0.5),
        'ffn_dw_b': nrm(ks[19], (DEPTH, 2 * D_FF), 0.02),
        'w_down': nrm(ks[20], (DEPTH, D_FF, D_MODEL), D_FF ** -0.5),
        'norm_final_w': 1.0 + nrm(ks[21], (D_MODEL,), 0.02),
    }


def reference(x, meta_tokens, norm_mix_w, w_in, b_gate, conf_dw_w, conf_dw_b, conf_ln_w,
              conf_ln_b, w_conf_out, dn_conv_w, dn_A_log, dn_dt_bias, dn_norm_w, w_dn_out,
              w_out, norm_ffn_w, w_up, ffn_dw_w, ffn_dw_b, w_down, norm_final_w):
    b = x.shape[0]
    meta = jnp.broadcast_to(meta_tokens[None].astype(x.dtype), (b, N_META, D_MODEL))
    h = jnp.concatenate([meta, x], axis=1)
    for i in range(DEPTH):
        h = hybrid_layer(h, norm_mix_w[i], w_in[i], b_gate[i], conf_dw_w[i], conf_dw_b[i],
                         conf_ln_w[i], conf_ln_b[i], w_conf_out[i], dn_conv_w[i], dn_A_log[i],
                         dn_dt_bias[i], dn_norm_w[i], w_dn_out[i], w_out[i], norm_ffn_w[i],
                         w_up[i], ffn_dw_w[i], ffn_dw_b[i], w_down[i])
    return rmsnorm(h, norm_final_w)[:, N_META:]
```

```python
import functools

import jax
import jax.numpy as jnp
from jax import lax
from jax.experimental import pallas as pl
from jax.experimental.pallas import tpu as pltpu

D_MODEL = 1024
N_META = 16
CHUNK = 64
PAD = CHUNK - N_META
DC = D_MODEL
CONF_K = 31
HEADS = 8
DK = 128
DV = 128
HK = HEADS * DK
HV = HEADS * DV
DN_CONV_K = 4
D_FF = 2816
FFN_CONV_K = 3
EPS = 1e-6

LANE = 128
SUBLANE = 8
TILE_ROWS = 5 * CHUNK
FF_BLOCK = 11 * LANE
VMEM_LIMIT = 56 * 1024 * 1024

F32 = jnp.float32
BF16 = jnp.bfloat16


def _bdot(a, b):
    return jnp.dot(a.astype(BF16), b.astype(BF16), preferred_element_type=F32)


def _hdot(a, b):
    return jnp.dot(a, b, preferred_element_type=F32, precision=lax.Precision.HIGHEST)


def _sigmoid(x):
    return 1.0 / (1.0 + jnp.exp(-x))


def _silu(x):
    return x * _sigmoid(x)


def _softplus(x):
    return jnp.maximum(x, 0.0) + jnp.log(1.0 + jnp.exp(-jnp.abs(x)))


def _rmsnorm(x, w):
    return x * lax.rsqrt(jnp.mean(x * x, axis=-1, keepdims=True) + EPS) * w


def _inproj_kernel(h_ref, nw_ref, wc_ref, wqkv_ref, wz_ref, wgb_ref, wgate_ref, bgate_ref,
                   alog_ref, dtb_ref, c_ref, qkv_ref, z_ref, gb_ref, gates_ref):
    i = pl.program_id(1)
    u = _rmsnorm(h_ref[...], nw_ref[...]).astype(BF16)
    cin = jnp.dot(u, wc_ref[...], preferred_element_type=F32)
    c_ref[...] = cin[:, :DC] * _sigmoid(cin[:, DC:])
    qkv_ref[...] = jnp.dot(u, wqkv_ref[...], preferred_element_type=F32)
    z_ref[...] = jnp.dot(u, wz_ref[...], preferred_element_type=F32)
    gates_ref[...] = _sigmoid(jnp.dot(u, wgate_ref[...], preferred_element_type=F32) + bgate_ref[...])
    p = jnp.dot(u, wgb_ref[...], preferred_element_type=F32)
    g = -jnp.exp(alog_ref[...]) * _softplus(p + dtb_ref[...])
    lane = lax.broadcasted_iota(jnp.int32, p.shape, 1)
    row = lax.broadcasted_iota(jnp.int32, p.shape, 0) + i * TILE_ROWS
    val = jnp.where(lane < HEADS, g, jnp.where(lane < 2 * HEADS, _sigmoid(p), 0.0))
    gb_ref[...] = jnp.where(row >= PAD, val, 0.0)


CONF_HALO = 32


def _conf_kernel(c_ref, w_ref, b_ref, lnw_ref, lnb_ref, o_ref, cbuf, ybuf):
    i = pl.program_id(1)

    @pl.when(i == 0)
    def _():
        cbuf[0:CONF_HALO, :] = jnp.zeros((CONF_HALO, DC), F32)

    @pl.when(i > 0)
    def _():
        cbuf[0:CONF_HALO, :] = cbuf[TILE_ROWS:TILE_ROWS + CONF_HALO, :]

    cbuf[CONF_HALO:CONF_HALO + TILE_ROWS, :] = c_ref[...]
    first = CONF_HALO - (CONF_K - 1)
    rows = CHUNK
    for lb in range(DC // LANE):
        ls = slice(lb * LANE, (lb + 1) * LANE)
        for rb in range(TILE_ROWS // rows):
            acc = jnp.broadcast_to(b_ref[:, ls], (rows, LANE))
            for j in range(CONF_K):
                r0 = first + j + rb * rows
                acc = acc + w_ref[j:j + 1, ls] * cbuf[r0:r0 + rows, ls]
            ybuf[rb * rows:(rb + 1) * rows, ls] = acc
    y = ybuf[...]
    mu = jnp.mean(y, axis=-1, keepdims=True)
    yc = y - mu
    yn = yc * lax.rsqrt(jnp.mean(yc * yc, axis=-1, keepdims=True) + EPS)
    o_ref[...] = _silu(yn * lnw_ref[...] + lnb_ref[...]).astype(o_ref.dtype)


QKV_HALO = SUBLANE
N_LEVELS = 6


def _unit_lower_inverse(a, eye, masks_ref):
    t = eye - a * masks_ref[0]
    for lvl in range(1, N_LEVELS):
        off = a * masks_ref[lvl]
        t = t - _hdot(t, _hdot(off, t))
    return t


def _gdn_kernel(qkv_ref, z_ref, gb_ref, cw_ref, nw_ref, tri_ref, masks_ref, o_ref, xbuf, qkvs, state):
    i = pl.program_id(1)

    @pl.when(i == 0)
    def _():
        xbuf[0:QKV_HALO, :] = jnp.zeros((QKV_HALO, 2 * HK + HV), F32)
        state[...] = jnp.zeros(state.shape, F32)

    @pl.when(i > 0)
    def _():
        xbuf[0:QKV_HALO, :] = xbuf[TILE_ROWS:TILE_ROWS + QKV_HALO, :]

    xbuf[QKV_HALO:QKV_HALO + TILE_ROWS, :] = qkv_ref[...]
    first = QKV_HALO - (DN_CONV_K - 1)
    rows = CHUNK
    for lb in range((2 * HK + HV) // LANE):
        ls = slice(lb * LANE, (lb + 1) * LANE)
        for rb in range(TILE_ROWS // rows):
            r0 = first + rb * rows
            acc = cw_ref[0:1, ls] * xbuf[r0:r0 + rows, ls]
            for j in range(1, DN_CONV_K):
                acc = acc + cw_ref[j:j + 1, ls] * xbuf[r0 + j:r0 + j + rows, ls]
            y = _silu(acc)
            if lb < 2 * HEADS:
                y = y * lax.rsqrt(jnp.sum(y * y, axis=-1, keepdims=True) + EPS)
            if lb < HEADS:
                y = y * (DK ** -0.5)
            qkvs[rb * rows:(rb + 1) * rows, ls] = y

    ri = lax.broadcasted_iota(jnp.int32, (CHUNK, CHUNK), 0)
    ci = lax.broadcasted_iota(jnp.int32, (CHUNK, CHUNK), 1)

    def chunk_body(c, carry):
        r0 = pl.multiple_of(c * CHUNK, CHUNK)
        rs = pl.ds(r0, CHUNK)
        gbc = gb_ref[rs, :]
        gc = _hdot(tri_ref[...], gbc)
        gct = gc.T
        glast = gc[CHUNK - 1:CHUNK, :]
        eg = jnp.exp(gc)
        ekd = jnp.exp(glast - gc)
        ecd = jnp.exp(glast)
        incl = ri >= ci
        eye = jnp.where(ri == ci, 1.0, 0.0)
        for h in range(HEADS):
            q = qkvs[rs, h * DK:(h + 1) * DK]
            k = qkvs[rs, HK + h * DK:HK + (h + 1) * DK]
            v = qkvs[rs, 2 * HK + h * DV:2 * HK + (h + 1) * DV]
            bcol = gbc[:, HEADS + h:HEADS + h + 1]
            diff = gc[:, h:h + 1] - gct[h:h + 1, :]
            decay = jnp.exp(jnp.where(incl, diff, -jnp.inf))
            kb = k * bcol
            kk = lax.dot_general(kb.astype(BF16), k.astype(BF16), (((1,), (1,)), ((), ())),
                                 preferred_element_type=F32)
            a_kk = jnp.where(ri > ci, kk * decay, 0.0)
            qk = lax.dot_general(q.astype(BF16), k.astype(BF16), (((1,), (1,)), ((), ())),
                                 preferred_element_type=F32)
            a_qk = jnp.where(incl, qk * decay, 0.0)
            tinv = _unit_lower_inverse(a_kk, eye, masks_ref)
            egh = eg[:, h:h + 1]
            rhs = jnp.concatenate([v * bcol, kb * egh], axis=1)
            sol = _hdot(tinv, rhs)
            u = sol[:, :DV]
            w = sol[:, DV:]
            qd = q * egh
            kd = k * ekd[:, h:h + 1]
            s = state[h]
            ws_qs = _bdot(jnp.concatenate([w, qd], axis=0), s)
            v_new = u - ws_qs[:CHUNK]
            o = ws_qs[CHUNK:] + _bdot(a_qk, v_new)
            state[h] = s * ecd[:, h:h + 1] + _bdot(kd.T, v_new)
            on = _rmsnorm(o, nw_ref[...])
            zz = z_ref[rs, h * DV:(h + 1) * DV]
            o_ref[rs, h * DV:(h + 1) * DV] = (on * _silu(zz)).astype(o_ref.dtype)
        return carry

    lax.fori_loop(0, TILE_ROWS // CHUNK, chunk_body, 0)


FFN_HALO = SUBLANE


def _ffn_kernel(h_ref, c2_ref, o2_ref, gates_ref, wco_ref, wdo_ref, wout_ref, nfw_ref,
                wupg_ref, wupv_ref, dwg_ref, dwv_ref, dbg_ref, dbv_ref, wdown_ref, nlw_ref,
                out_ref, gbuf, vbuf, halo_g, halo_v):
    i = pl.program_id(1)
    yc = jnp.dot(c2_ref[...], wco_ref[...], preferred_element_type=F32)
    yd = jnp.dot(o2_ref[...], wdo_ref[...], preferred_element_type=F32)
    mix = gates_ref[:, :D_MODEL] * yc + gates_ref[:, D_MODEL:] * yd
    h1 = h_ref[...] + _bdot(mix, wout_ref[...])
    row = lax.broadcasted_iota(jnp.int32, h1.shape, 0) + i * TILE_ROWS
    h1 = jnp.where(row >= PAD, h1, 0.0)
    u2 = _rmsnorm(h1, nfw_ref[...]).astype(BF16)

    @pl.when(i == 0)
    def _():
        halo_g[...] = jnp.zeros(halo_g.shape, F32)
        halo_v[...] = jnp.zeros(halo_v.shape, F32)

    first = FFN_HALO - (FFN_CONV_K - 1)
    acc = h1
    for blk in range(D_FF // FF_BLOCK):
        cs = slice(blk * FF_BLOCK, (blk + 1) * FF_BLOCK)
        ys = []
        for buf, halo, wup, dw, db in ((gbuf, halo_g, wupg_ref, dwg_ref, dbg_ref),
                                      (vbuf, halo_v, wupv_ref, dwv_ref, dbv_ref)):
            buf[0:FFN_HALO, :] = halo[:, cs]
            buf[FFN_HALO:FFN_HALO + TILE_ROWS, :] = jnp.dot(u2, wup[:, cs], preferred_element_type=F32)
            halo[:, cs] = buf[TILE_ROWS:TILE_ROWS + FFN_HALO, :]
            y = db[:, cs] + dw[0:1, cs] * buf[first:first + TILE_ROWS, :]
            for j in range(1, FFN_CONV_K):
                y = y + dw[j:j + 1, cs] * buf[first + j:first + j + TILE_ROWS, :]
            ys.append(y)
        act = (_silu(ys[0]) * ys[1]).astype(BF16)
        acc = acc + jnp.dot(act, wdown_ref[cs, :], preferred_element_type=F32)
    out_ref[...] = _rmsnorm(acc, nlw_ref[...])


def _const_spec(shape):
    nd = len(shape)
    return pl.BlockSpec(shape, lambda b, i: (0,) * nd, pipeline_mode=pl.Buffered(1))


def _tile_spec(width):
    return pl.BlockSpec((None, TILE_ROWS, width), lambda b, i: (b, i, 0))


def _params(sem):
    return pltpu.CompilerParams(dimension_semantics=sem, vmem_limit_bytes=VMEM_LIMIT)


def _level_masks():
    r = jnp.arange(CHUNK)[:, None]
    c = jnp.arange(CHUNK)[None, :]
    masks = []
    for lvl in range(N_LEVELS):
        s = 1 << lvl
        m = ((r // (2 * s)) == (c // (2 * s))) & ((r // s) % 2 == 1) & ((c // s) % 2 == 0)
        masks.append(m)
    return jnp.stack(masks).astype(F32)


def kernel(x, meta_tokens, norm_mix_w, w_in, b_gate, conf_dw_w, conf_dw_b, conf_ln_w, conf_ln_b, w_conf_out, dn_conv_w, dn_A_log, dn_dt_bias, dn_norm_w, w_dn_out, w_out, norm_ffn_w, w_up, ffn_dw_w, ffn_dw_b, w_down, norm_final_w):
    bsz, seq, _ = x.shape
    lp = PAD + N_META + seq
    assert lp % TILE_ROWS == 0, "padded sequence must be a whole number of tiles"
    assert w_in.shape[0] == 1, "single-layer block"
    nt = lp // TILE_ROWS
    grid = (bsz, nt)

    meta = jnp.broadcast_to(meta_tokens[None].astype(x.dtype), (bsz, N_META, D_MODEL))
    hp = jnp.concatenate([jnp.zeros((bsz, PAD, D_MODEL), x.dtype), meta, x], axis=1)

    wi = w_in[0]
    o_q = 2 * DC
    o_z = o_q + 2 * HK + HV
    o_a = o_z + HV
    o_gate = o_a + 2 * HEADS
    w_c = wi[:, :o_q].astype(BF16)
    w_qkv = wi[:, o_q:o_z].astype(BF16)
    w_z = wi[:, o_z:o_a].astype(BF16)
    w_gb = jnp.pad(wi[:, o_a:o_gate], ((0, 0), (0, LANE - 2 * HEADS))).astype(BF16)
    w_gate = wi[:, o_gate:].astype(BF16)
    row = lambda v: v.reshape(1, -1).astype(F32)
    alog = jnp.pad(row(dn_A_log[0]), ((0, 0), (0, LANE - HEADS)))
    dtb = jnp.pad(row(dn_dt_bias[0]), ((0, 0), (0, LANE - HEADS)))

    c, qkv, z, gb, gates = pl.pallas_call(
        _inproj_kernel,
        grid=grid,
        in_specs=[_tile_spec(D_MODEL), _const_spec((1, D_MODEL)), _const_spec(w_c.shape),
                  _const_spec(w_qkv.shape), _const_spec(w_z.shape), _const_spec(w_gb.shape),
                  _const_spec(w_gate.shape), _const_spec((1, 2 * D_MODEL)),
                  _const_spec((1, LANE)), _const_spec((1, LANE))],
        out_specs=[_tile_spec(DC), _tile_spec(2 * HK + HV), _tile_spec(HV), _tile_spec(LANE),
                   _tile_spec(2 * D_MODEL)],
        out_shape=[jax.ShapeDtypeStruct((bsz, lp, DC), F32),
                   jax.ShapeDtypeStruct((bsz, lp, 2 * HK + HV), F32),
                   jax.ShapeDtypeStruct((bsz, lp, HV), F32),
                   jax.ShapeDtypeStruct((bsz, lp, LANE), F32),
                   jax.ShapeDtypeStruct((bsz, lp, 2 * D_MODEL), F32)],
        compiler_params=_params(("parallel", "parallel")),
        name="inproj",
    )(hp, row(norm_mix_w[0]), w_c, w_qkv, w_z, w_gb, w_gate, row(b_gate[0]), alog, dtb)

    c2 = pl.pallas_call(
        _conf_kernel,
        grid=grid,
        in_specs=[_tile_spec(DC), _const_spec((CONF_K, DC)), _const_spec((1, DC)),
                  _const_spec((1, DC)), _const_spec((1, DC))],
        out_specs=_tile_spec(DC),
        out_shape=jax.ShapeDtypeStruct((bsz, lp, DC), BF16),
        scratch_shapes=[pltpu.VMEM((TILE_ROWS + CONF_HALO, DC), F32),
                        pltpu.VMEM((TILE_ROWS, DC), F32)],
        compiler_params=_params(("arbitrary", "arbitrary")),
        name="conf_conv",
    )(c, conf_dw_w[0].astype(F32), row(conf_dw_b[0]), row(conf_ln_w[0]), row(conf_ln_b[0]))

    tri = (jnp.arange(CHUNK)[:, None] >= jnp.arange(CHUNK)[None, :]).astype(F32)
    o2 = pl.pallas_call(
        _gdn_kernel,
        grid=grid,
        in_specs=[_tile_spec(2 * HK + HV), _tile_spec(HV), _tile_spec(LANE),
                  _const_spec((DN_CONV_K, 2 * HK + HV)), _const_spec((1, DV)),
                  _const_spec((CHUNK, CHUNK)), _const_spec((N_LEVELS, CHUNK, CHUNK))],
        out_specs=_tile_spec(HV),
        out_shape=jax.ShapeDtypeStruct((bsz, lp, HV), BF16),
        scratch_shapes=[pltpu.VMEM((TILE_ROWS + QKV_HALO, 2 * HK + HV), F32),
                        pltpu.VMEM((TILE_ROWS, 2 * HK + HV), F32),
                        pltpu.VMEM((HEADS, DK, DV), F32)],
        compiler_params=_params(("arbitrary", "arbitrary")),
        name="gdn",
    )(qkv, z, gb, dn_conv_w[0].astype(F32), row(dn_norm_w[0]), tri, _level_masks())

    wu = w_up[0]
    out_p = pl.pallas_call(
        _ffn_kernel,
        grid=grid,
        in_specs=[_tile_spec(D_MODEL), _tile_spec(DC), _tile_spec(HV), _tile_spec(2 * D_MODEL),
                  _const_spec((DC, D_MODEL)), _const_spec((HV, D_MODEL)),
                  _const_spec((D_MODEL, D_MODEL)), _const_spec((1, D_MODEL)),
                  _const_spec((D_MODEL, D_FF)), _const_spec((D_MODEL, D_FF)),
                  _const_spec((FFN_CONV_K, D_FF)), _const_spec((FFN_CONV_K, D_FF)),
                  _const_spec((1, D_FF)), _const_spec((1, D_FF)),
                  _const_spec((D_FF, D_MODEL)), _const_spec((1, D_MODEL))],
        out_specs=_tile_spec(D_MODEL),
        out_shape=jax.ShapeDtypeStruct((bsz, lp, D_MODEL), F32),
        scratch_shapes=[pltpu.VMEM((TILE_ROWS + FFN_HALO, FF_BLOCK), F32),
                        pltpu.VMEM((TILE_ROWS + FFN_HALO, FF_BLOCK), F32),
                        pltpu.VMEM((FFN_HALO, D_FF), F32),
                        pltpu.VMEM((FFN_HALO, D_FF), F32)],
        compiler_params=_params(("arbitrary", "arbitrary")),
        name="merge_ffn",
    )(hp, c2, o2, gates, w_conf_out[0].astype(BF16), w_dn_out[0].astype(BF16),
      w_out[0].astype(BF16), row(norm_ffn_w[0]), wu[:, :D_FF].astype(BF16),
      wu[:, D_FF:].astype(BF16), ffn_dw_w[0][:, :D_FF].astype(F32), ffn_dw_w[0][:, D_FF:].astype(F32),
      row(ffn_dw_b[0][:D_FF]), row(ffn_dw_b[0][D_FF:]), w_down[0].astype(BF16), row(norm_final_w))

    return out_p[:, PAD + N_META:]
```

```python
import functools

import jax
import jax.numpy as jnp
from jax import lax
from jax.experimental import pallas as pl
from jax.experimental.pallas import tpu as pltpu

D_MODEL = 1024
N_META = 16
CHUNK = 64
PAD = CHUNK - N_META
DC = D_MODEL
CONF_K = 31
HEADS = 8
DK = 128
DV = 128
HK = HEADS * DK
HV = HEADS * DV
DN_CONV_K = 4
D_FF = 2816
FFN_CONV_K = 3
EPS = 1e-6

LANE = 128
SUBLANE = 8
TILE_ROWS = 5 * CHUNK
FF_BLOCK = 11 * LANE
VMEM_LIMIT = 56 * 1024 * 1024

F32 = jnp.float32
BF16 = jnp.bfloat16


def _bdot(a, b):
    return jnp.dot(a.astype(BF16), b.astype(BF16), preferred_element_type=F32)


def _sigmoid(x):
    return 1.0 / (1.0 + jnp.exp(-x))


def _silu(x):
    return x * _sigmoid(x)


def _softplus(x):
    return jnp.maximum(x, 0.0) + jnp.log(1.0 + jnp.exp(-jnp.abs(x)))


def _rmsnorm(x, w):
    return x * lax.rsqrt(jnp.mean(x * x, axis=-1, keepdims=True) + EPS) * w


def _inproj_kernel(h_ref, nw_ref, wc_ref, wqkv_ref, wz_ref, wgb_ref, wgate_ref, bgate_ref,
                   alog_ref, dtb_ref, c_ref, qkv_ref, z_ref, gb_ref, gates_ref):
    i = pl.program_id(1)
    u = _rmsnorm(h_ref[...], nw_ref[...]).astype(BF16)
    cin = jnp.dot(u, wc_ref[...], preferred_element_type=F32)
    c_ref[...] = cin[:, :DC] * _sigmoid(cin[:, DC:])
    qkv_ref[...] = jnp.dot(u, wqkv_ref[...], preferred_element_type=F32)
    z_ref[...] = jnp.dot(u, wz_ref[...], preferred_element_type=F32)
    gates_ref[...] = _sigmoid(jnp.dot(u, wgate_ref[...], preferred_element_type=F32) + bgate_ref[...])
    p = jnp.dot(u, wgb_ref[...], preferred_element_type=F32)
    g = -jnp.exp(alog_ref[...]) * _softplus(p + dtb_ref[...])
    lane = lax.broadcasted_iota(jnp.int32, p.shape, 1)
    row = lax.broadcasted_iota(jnp.int32, p.shape, 0) + i * TILE_ROWS
    val = jnp.where(lane < HEADS, g, jnp.where(lane < 2 * HEADS, _sigmoid(p), 0.0))
    gb_ref[...] = jnp.where(row >= PAD, val, 0.0)


CONF_HALO = 32


def _conf_kernel(c_ref, w_ref, b_ref, lnw_ref, lnb_ref, o_ref, cbuf, ybuf):
    i = pl.program_id(1)

    @pl.when(i == 0)
    def _():
        cbuf[0:CONF_HALO, :] = jnp.zeros((CONF_HALO, DC), F32)

    @pl.when(i > 0)
    def _():
        cbuf[0:CONF_HALO, :] = cbuf[TILE_ROWS:TILE_ROWS + CONF_HALO, :]

    cbuf[CONF_HALO:CONF_HALO + TILE_ROWS, :] = c_ref[...]
    first = CONF_HALO - (CONF_K - 1)
    rows = CHUNK
    for lb in range(DC // LANE):
        ls = slice(lb * LANE, (lb + 1) * LANE)
        for rb in range(TILE_ROWS // rows):
            acc = jnp.broadcast_to(b_ref[:, ls], (rows, LANE))
            for j in range(CONF_K):
                r0 = first + j + rb * rows
                acc = acc + w_ref[j:j + 1, ls] * cbuf[r0:r0 + rows, ls]
            ybuf[rb * rows:(rb + 1) * rows, ls] = acc
    y = ybuf[...]
    mu = jnp.mean(y, axis=-1, keepdims=True)
    yc = y - mu
    yn = yc * lax.rsqrt(jnp.mean(yc * yc, axis=-1, keepdims=True) + EPS)
    o_ref[...] = _silu(yn * lnw_ref[...] + lnb_ref[...]).astype(o_ref.dtype)


QKV_HALO = SUBLANE
N_LEVELS = 6


def _split3(x):
    hi = x.astype(BF16)
    r = x - hi.astype(F32)
    mid = r.astype(BF16)
    lo = (r - mid.astype(F32)).astype(BF16)
    return hi, mid, lo


def _gdn_kernel(qkv_ref, z_ref, gb_ref, cw_ref, nw_ref, tri_ref, masks_ref, o_ref, xbuf, qkvs, state):
    i = pl.program_id(1)

    @pl.when(i == 0)
    def _():
        xbuf[0:QKV_HALO, :] = jnp.zeros((QKV_HALO, 2 * HK + HV), F32)
        state[...] = jnp.zeros(state.shape, F32)

    @pl.when(i > 0)
    def _():
        xbuf[0:QKV_HALO, :] = xbuf[TILE_ROWS:TILE_ROWS + QKV_HALO, :]

    xbuf[QKV_HALO:QKV_HALO + TILE_ROWS, :] = qkv_ref[...]
    first = QKV_HALO - (DN_CONV_K - 1)
    rows = CHUNK
    for lb in range((2 * HK + HV) // LANE):
        ls = slice(lb * LANE, (lb + 1) * LANE)
        for rb in range(TILE_ROWS // rows):
            r0 = first + rb * rows
            acc = cw_ref[0:1, ls] * xbuf[r0:r0 + rows, ls]
            for j in range(1, DN_CONV_K):
                acc = acc + cw_ref[j:j + 1, ls] * xbuf[r0 + j:r0 + j + rows, ls]
            y = _silu(acc)
            if lb < 2 * HEADS:
                y = y * lax.rsqrt(jnp.sum(y * y, axis=-1, keepdims=True) + EPS)
            if lb < HEADS:
                y = y * (DK ** -0.5)
            qkvs[rb * rows:(rb + 1) * rows, ls] = y

    ri = lax.broadcasted_iota(jnp.int32, (CHUNK, CHUNK), 0)
    ci = lax.broadcasted_iota(jnp.int32, (CHUNK, CHUNK), 1)

    def chunk_body(c, carry):
        r0 = pl.multiple_of(c * CHUNK, CHUNK)
        rs = pl.ds(r0, CHUNK)
        hs = range(HEADS)
        gbc = gb_ref[rs, :]
        tri = tri_ref[...]
        gc = sum(jnp.dot(tri, p, preferred_element_type=F32) for p in _split3(gbc))
        gct = gc.T
        glast = gc[CHUNK - 1:CHUNK, :]
        eg = jnp.exp(gc)
        ekd_t = jnp.exp(gct[:, CHUNK - 1:CHUNK] - gct)
        ecd = jnp.exp(glast)
        incl = ri >= ci
        strict = ri > ci
        eye = jnp.where(ri == ci, 1.0, 0.0)
        q = [qkvs[rs, h * DK:(h + 1) * DK] for h in hs]
        k = [qkvs[rs, HK + h * DK:HK + (h + 1) * DK] for h in hs]
        v = [qkvs[rs, 2 * HK + h * DV:2 * HK + (h + 1) * DV] for h in hs]
        kt = [k[h].T for h in hs]
        bcol = [gbc[:, HEADS + h:HEADS + h + 1] for h in hs]
        kb = [k[h] * bcol[h] for h in hs]
        ktb = [kt[h].astype(BF16) for h in hs]
        kk = [_bdot(kb[h], ktb[h]) for h in hs]
        qk = [_bdot(q[h], ktb[h]) for h in hs]
        decay = [jnp.exp(jnp.where(incl, gc[:, h:h + 1] - gct[h:h + 1, :], -jnp.inf)) for h in hs]
        a_kk = [jnp.where(strict, kk[h] * decay[h], 0.0) for h in hs]
        a_qk = [jnp.where(incl, qk[h] * decay[h], 0.0) for h in hs]
        t = [eye - a_kk[h] * masks_ref[0] for h in hs]
        for lvl in range(1, N_LEVELS):
            tb = [t[h].astype(BF16) for h in hs]
            x = [_bdot(a_kk[h] * masks_ref[lvl], tb[h]) for h in hs]
            t = [t[h] - _bdot(tb[h], x[h]) for h in hs]
        egh = [eg[:, h:h + 1] for h in hs]
        sol = [_bdot(t[h], jnp.concatenate([v[h] * bcol[h], kb[h] * egh[h]], axis=1)) for h in hs]
        s = [state[h] for h in hs]
        ws_qs = [_bdot(jnp.concatenate([sol[h][:, DV:], q[h] * egh[h]], axis=0), s[h]) for h in hs]
        v_new = [sol[h][:, :DV] - ws_qs[h][:CHUNK] for h in hs]
        o = [ws_qs[h][CHUNK:] + _bdot(a_qk[h], v_new[h]) for h in hs]
        for h in hs:
            kdt = kt[h] * ekd_t[h:h + 1, :]
            state[h] = s[h] * ecd[:, h:h + 1] + _bdot(kdt, v_new[h])
        for h in hs:
            on = _rmsnorm(o[h], nw_ref[...])
            zz = z_ref[rs, h * DV:(h + 1) * DV]
            o_ref[rs, h * DV:(h + 1) * DV] = (on * _silu(zz)).astype(o_ref.dtype)
        return carry

    lax.fori_loop(0, TILE_ROWS // CHUNK, chunk_body, 0)


FFN_HALO = SUBLANE


def _ffn_kernel(h_ref, c2_ref, o2_ref, gates_ref, wco_ref, wdo_ref, wout_ref, nfw_ref,
                wupg_ref, wupv_ref, dwg_ref, dwv_ref, dbg_ref, dbv_ref, wdown_ref, nlw_ref,
                out_ref, gbuf, vbuf, halo_g, halo_v):
    i = pl.program_id(1)
    yc = jnp.dot(c2_ref[...], wco_ref[...], preferred_element_type=F32)
    yd = jnp.dot(o2_ref[...], wdo_ref[...], preferred_element_type=F32)
    mix = gates_ref[:, :D_MODEL] * yc + gates_ref[:, D_MODEL:] * yd
    h1 = h_ref[...] + _bdot(mix, wout_ref[...])
    row = lax.broadcasted_iota(jnp.int32, h1.shape, 0) + i * TILE_ROWS
    h1 = jnp.where(row >= PAD, h1, 0.0)
    u2 = _rmsnorm(h1, nfw_ref[...]).astype(BF16)

    @pl.when(i == 0)
    def _():
        halo_g[...] = jnp.zeros(halo_g.shape, F32)
        halo_v[...] = jnp.zeros(halo_v.shape, F32)

    first = FFN_HALO - (FFN_CONV_K - 1)
    acc = h1
    for blk in range(D_FF // FF_BLOCK):
        cs = slice(blk * FF_BLOCK, (blk + 1) * FF_BLOCK)
        ys = []
        for buf, halo, wup, dw, db in ((gbuf, halo_g, wupg_ref, dwg_ref, dbg_ref),
                                      (vbuf, halo_v, wupv_ref, dwv_ref, dbv_ref)):
            buf[0:FFN_HALO, :] = halo[:, cs]
            buf[FFN_HALO:FFN_HALO + TILE_ROWS, :] = jnp.dot(u2, wup[:, cs], preferred_element_type=F32)
            halo[:, cs] = buf[TILE_ROWS:TILE_ROWS + FFN_HALO, :]
            y = db[:, cs] + dw[0:1, cs] * buf[first:first + TILE_ROWS, :]
            for j in range(1, FFN_CONV_K):
                y = y + dw[j:j + 1, cs] * buf[first + j:first + j + TILE_ROWS, :]
            ys.append(y)
        act = (_silu(ys[0]) * ys[1]).astype(BF16)
        acc = acc + jnp.dot(act, wdown_ref[cs, :], preferred_element_type=F32)
    out_ref[...] = _rmsnorm(acc, nlw_ref[...])


def _const_spec(shape):
    nd = len(shape)
    return pl.BlockSpec(shape, lambda b, i: (0,) * nd, pipeline_mode=pl.Buffered(1))


def _tile_spec(width):
    return pl.BlockSpec((None, TILE_ROWS, width), lambda b, i: (b, i, 0))


def _params(sem):
    return pltpu.CompilerParams(dimension_semantics=sem, vmem_limit_bytes=VMEM_LIMIT)


def _level_masks():
    r = jnp.arange(CHUNK)[:, None]
    c = jnp.arange(CHUNK)[None, :]
    masks = []
    for lvl in range(N_LEVELS):
        s = 1 << lvl
        m = ((r // (2 * s)) == (c // (2 * s))) & ((r // s) % 2 == 1) & ((c // s) % 2 == 0)
        masks.append(m)
    return jnp.stack(masks).astype(F32)


def kernel(x, meta_tokens, norm_mix_w, w_in, b_gate, conf_dw_w, conf_dw_b, conf_ln_w, conf_ln_b, w_conf_out, dn_conv_w, dn_A_log, dn_dt_bias, dn_norm_w, w_dn_out, w_out, norm_ffn_w, w_up, ffn_dw_w, ffn_dw_b, w_down, norm_final_w):
    bsz, seq, _ = x.shape
    lp = PAD + N_META + seq
    assert lp % TILE_ROWS == 0, "padded sequence must be a whole number of tiles"
    assert w_in.shape[0] == 1, "single-layer block"
    nt = lp // TILE_ROWS
    grid = (bsz, nt)

    meta = jnp.broadcast_to(meta_tokens[None].astype(x.dtype), (bsz, N_META, D_MODEL))
    hp = jnp.concatenate([jnp.zeros((bsz, PAD, D_MODEL), x.dtype), meta, x], axis=1)

    wi = w_in[0]
    o_q = 2 * DC
    o_z = o_q + 2 * HK + HV
    o_a = o_z + HV
    o_gate = o_a + 2 * HEADS
    w_c = wi[:, :o_q].astype(BF16)
    w_qkv = wi[:, o_q:o_z].astype(BF16)
    w_z = wi[:, o_z:o_a].astype(BF16)
    w_gb = jnp.pad(wi[:, o_a:o_gate], ((0, 0), (0, LANE - 2 * HEADS))).astype(BF16)
    w_gate = wi[:, o_gate:].astype(BF16)
    row = lambda v: v.reshape(1, -1).astype(F32)
    alog = jnp.pad(row(dn_A_log[0]), ((0, 0), (0, LANE - HEADS)))
    dtb = jnp.pad(row(dn_dt_bias[0]), ((0, 0), (0, LANE - HEADS)))

    c, qkv, z, gb, gates = pl.pallas_call(
        _inproj_kernel,
        grid=grid,
        in_specs=[_tile_spec(D_MODEL), _const_spec((1, D_MODEL)), _const_spec(w_c.shape),
                  _const_spec(w_qkv.shape), _const_spec(w_z.shape), _const_spec(w_gb.shape),
                  _const_spec(w_gate.shape), _const_spec((1, 2 * D_MODEL)),
                  _const_spec((1, LANE)), _const_spec((1, LANE))],
        out_specs=[_tile_spec(DC), _tile_spec(2 * HK + HV), _tile_spec(HV), _tile_spec(LANE),
                   _tile_spec(2 * D_MODEL)],
        out_shape=[jax.ShapeDtypeStruct((bsz, lp, DC), F32),
                   jax.ShapeDtypeStruct((bsz, lp, 2 * HK + HV), F32),
                   jax.ShapeDtypeStruct((bsz, lp, HV), F32),
                   jax.ShapeDtypeStruct((bsz, lp, LANE), F32),
                   jax.ShapeDtypeStruct((bsz, lp, 2 * D_MODEL), F32)],
        compiler_params=_params(("parallel", "parallel")),
        name="inproj",
    )(hp, row(norm_mix_w[0]), w_c, w_qkv, w_z, w_gb, w_gate, row(b_gate[0]), alog, dtb)

    c2 = pl.pallas_call(
        _conf_kernel,
        grid=grid,
        in_specs=[_tile_spec(DC), _const_spec((CONF_K, DC)), _const_spec((1, DC)),
                  _const_spec((1, DC)), _const_spec((1, DC))],
        out_specs=_tile_spec(DC),
        out_shape=jax.ShapeDtypeStruct((bsz, lp, DC), BF16),
        scratch_shapes=[pltpu.VMEM((TILE_ROWS + CONF_HALO, DC), F32),
                        pltpu.VMEM((TILE_ROWS, DC), F32)],
        compiler_params=_params(("arbitrary", "arbitrary")),
        name="conf_conv",
    )(c, conf_dw_w[0].astype(F32), row(conf_dw_b[0]), row(conf_ln_w[0]), row(conf_ln_b[0]))

    tri = (jnp.arange(CHUNK)[:, None] >= jnp.arange(CHUNK)[None, :]).astype(BF16)
    o2 = pl.pallas_call(
        _gdn_kernel,
        grid=grid,
        in_specs=[_tile_spec(2 * HK + HV), _tile_spec(HV), _tile_spec(LANE),
                  _const_spec((DN_CONV_K, 2 * HK + HV)), _const_spec((1, DV)),
                  _const_spec((CHUNK, CHUNK)), _const_spec((N_LEVELS, CHUNK, CHUNK))],
        out_specs=_tile_spec(HV),
        out_shape=jax.ShapeDtypeStruct((bsz, lp, HV), BF16),
        scratch_shapes=[pltpu.VMEM((TILE_ROWS + QKV_HALO, 2 * HK + HV), F32),
                        pltpu.VMEM((TILE_ROWS, 2 * HK + HV), F32),
                        pltpu.VMEM((HEADS, DK, DV), F32)],
        compiler_params=_params(("arbitrary", "arbitrary")),
        name="gdn",
    )(qkv, z, gb, dn_conv_w[0].astype(F32), row(dn_norm_w[0]), tri, _level_masks())

    wu = w_up[0]
    out_p = pl.pallas_call(
        _ffn_kernel,
        grid=grid,
        in_specs=[_tile_spec(D_MODEL), _tile_spec(DC), _tile_spec(HV), _tile_spec(2 * D_MODEL),
                  _const_spec((DC, D_MODEL)), _const_spec((HV, D_MODEL)),
                  _const_spec((D_MODEL, D_MODEL)), _const_spec((1, D_MODEL)),
                  _const_spec((D_MODEL, D_FF)), _const_spec((D_MODEL, D_FF)),
                  _const_spec((FFN_CONV_K, D_FF)), _const_spec((FFN_CONV_K, D_FF)),
                  _const_spec((1, D_FF)), _const_spec((1, D_FF)),
                  _const_spec((D_FF, D_MODEL)), _const_spec((1, D_MODEL))],
        out_specs=_tile_spec(D_MODEL),
        out_shape=jax.ShapeDtypeStruct((bsz, lp, D_MODEL), F32),
        scratch_shapes=[pltpu.VMEM((TILE_ROWS + FFN_HALO, FF_BLOCK), F32),
                        pltpu.VMEM((TILE_ROWS + FFN_HALO, FF_BLOCK), F32),
                        pltpu.VMEM((FFN_HALO, D_FF), F32),
                        pltpu.VMEM((FFN_HALO, D_FF), F32)],
        compiler_params=_params(("arbitrary", "arbitrary")),
        name="merge_ffn",
    )(hp, c2, o2, gates, w_conf_out[0].astype(BF16), w_dn_out[0].astype(BF16),
      w_out[0].astype(BF16), row(norm_ffn_w[0]), wu[:, :D_FF].astype(BF16),
      wu[:, D_FF:].astype(BF16), ffn_dw_w[0][:, :D_FF].astype(F32), ffn_dw_w[0][:, D_FF:].astype(F32),
      row(ffn_dw_b[0][:D_FF]), row(ffn_dw_b[0][D_FF:]), w_down[0].astype(BF16), row(norm_final_w))

    return out_p[:, PAD + N_META:]
```

```python
import functools

import jax
import jax.numpy as jnp
from jax import lax
from jax.experimental import pallas as pl
from jax.experimental.pallas import tpu as pltpu

D_MODEL = 1024
N_META = 16
CHUNK = 64
PAD = CHUNK - N_META
DC = D_MODEL
CONF_K = 31
HEADS = 8
DK = 128
DV = 128
HK = HEADS * DK
HV = HEADS * DV
QKV = 2 * HK + HV
DN_CONV_K = 4
D_FF = 2816
FFN_CONV_K = 3
EPS = 1e-6

LANE = 128
SUBLANE = 8
BODY_ROWS = 512
FF_BLOCK = 11 * LANE
CHUNK_GROUP = 4
VMEM_LIMIT = 56 * 1024 * 1024

CONF_HALO = 32
QKV_HALO = SUBLANE
FFN_HALO = SUBLANE
N_LEVELS = 6

F32 = jnp.float32
BF16 = jnp.bfloat16


def _bdot(a, b):
    return jnp.dot(a.astype(BF16), b.astype(BF16), preferred_element_type=F32)


def _sigmoid(x):
    return 1.0 / (1.0 + jnp.exp(-x))


def _silu(x):
    return x * _sigmoid(x)


def _softplus(x):
    return jnp.maximum(x, 0.0) + jnp.log(1.0 + jnp.exp(-jnp.abs(x)))


def _rmsnorm(x, w):
    return x * lax.rsqrt(jnp.mean(x * x, axis=-1, keepdims=True) + EPS) * w


def _split3(x):
    hi = x.astype(BF16)
    r = x - hi.astype(F32)
    mid = r.astype(BF16)
    lo = (r - mid.astype(F32)).astype(BF16)
    return hi, mid, lo


def _pad_row_mask(shape):
    return lax.broadcasted_iota(jnp.int32, shape, 0) >= PAD


def _inproj_kernel(h_ref, nw_ref, wc_ref, wqkv_ref, wgb_ref, alog_ref, dtb_ref,
                   c_ref, qkv_ref, gb_ref, *, head_pass):
    u = _rmsnorm(h_ref[...], nw_ref[...]).astype(BF16)
    cin = jnp.dot(u, wc_ref[...], preferred_element_type=F32)
    c_ref[...] = cin[:, :DC] * _sigmoid(cin[:, DC:])
    qkv_ref[...] = jnp.dot(u, wqkv_ref[...], preferred_element_type=F32)
    p = jnp.dot(u, wgb_ref[...], preferred_element_type=F32)
    g = -jnp.exp(alog_ref[...]) * _softplus(p + dtb_ref[...])
    lane = lax.broadcasted_iota(jnp.int32, p.shape, 1)
    val = jnp.where(lane < HEADS, g, jnp.where(lane < 2 * HEADS, _sigmoid(p), 0.0))
    if head_pass:
        val = jnp.where(_pad_row_mask(p.shape), val, 0.0)
    gb_ref[...] = val


def _conf_kernel(c_ref, halo_ref, w_ref, b_ref, lnw_ref, lnb_ref, o_ref, cbuf, ybuf, *, tl):
    i = pl.program_id(1)

    @pl.when(i == 0)
    def _():
        cbuf[0:CONF_HALO, :] = halo_ref[...]

    @pl.when(i > 0)
    def _():
        cbuf[0:CONF_HALO, :] = cbuf[tl:tl + CONF_HALO, :]

    cbuf[CONF_HALO:CONF_HALO + tl, :] = c_ref[...]
    first = CONF_HALO - (CONF_K - 1)
    rows = CHUNK
    for lb in range(DC // LANE):
        ls = slice(lb * LANE, (lb + 1) * LANE)
        for rb in range(tl // rows):
            base = rb * rows
            y = jnp.broadcast_to(b_ref[:, ls], (rows, LANE))
            for r in range(SUBLANE):
                span = rows + (SUBLANE if r else 0)
                z = None
                for j in range(CONF_K):
                    if (first + j) % SUBLANE != r:
                        continue
                    a = base + first + j - r
                    term = w_ref[j:j + 1, ls] * cbuf[a:a + span, ls]
                    z = term if z is None else z + term
                if z is not None:
                    y = y + z[r:r + rows]
            ybuf[base:base + rows, ls] = y
    y = ybuf[...]
    mu = jnp.mean(y, axis=-1, keepdims=True)
    yc = y - mu
    yn = yc * lax.rsqrt(jnp.mean(yc * yc, axis=-1, keepdims=True) + EPS)
    o_ref[...] = _silu(yn * lnw_ref[...] + lnb_ref[...]).astype(o_ref.dtype)


def _gdn_kernel(qkv_ref, h_ref, gb_ref, halo_ref, s0_ref, cw_ref, nw_ref, nmw_ref, wz_ref,
                tri_ref, masks_ref, o_ref, sout_ref,
                xbuf, qkvs, zbuf, wq_s, u_s, aqk_s, kdt_s, ecd_s, state, *, tl, group):
    i = pl.program_id(1)
    nc = tl // CHUNK

    @pl.when(i == 0)
    def _():
        xbuf[0:QKV_HALO, :] = halo_ref[...]
        state[...] = s0_ref[...]

    @pl.when(i > 0)
    def _():
        xbuf[0:QKV_HALO, :] = xbuf[tl:tl + QKV_HALO, :]

    xbuf[QKV_HALO:QKV_HALO + tl, :] = qkv_ref[...]
    first = QKV_HALO - (DN_CONV_K - 1)
    rows = CHUNK
    for lb in range(QKV // LANE):
        ls = slice(lb * LANE, (lb + 1) * LANE)
        for rb in range(tl // rows):
            r0 = first + rb * rows
            acc = cw_ref[0:1, ls] * xbuf[r0:r0 + rows, ls]
            for j in range(1, DN_CONV_K):
                acc = acc + cw_ref[j:j + 1, ls] * xbuf[r0 + j:r0 + j + rows, ls]
            y = _silu(acc)
            if lb < 2 * HEADS:
                y = y * lax.rsqrt(jnp.sum(y * y, axis=-1, keepdims=True) + EPS)
            if lb < HEADS:
                y = y * (DK ** -0.5)
            qkvs[rb * rows:(rb + 1) * rows, ls] = y

    zbuf[...] = _silu(jnp.dot(_rmsnorm(h_ref[...], nmw_ref[...]).astype(BF16), wz_ref[...],
                              preferred_element_type=F32))

    ri = lax.broadcasted_iota(jnp.int32, (CHUNK, CHUNK), 0)
    ci = lax.broadcasted_iota(jnp.int32, (CHUNK, CHUNK), 1)

    def prep_body(g, carry):
        probs = [(j, h) for j in range(group) for h in range(HEADS)]
        incl = ri >= ci
        strict = ri > ci
        eye = jnp.where(ri == ci, 1.0, 0.0)
        tri = tri_ref[...]
        gbc, gc, gct, eg, ekd_t = [], [], [], [], []
        for j in range(group):
            rs = pl.ds(pl.multiple_of((g * group + j) * CHUNK, CHUNK), CHUNK)
            gb_j = gb_ref[rs, :]
            gc_j = sum(jnp.dot(tri, p, preferred_element_type=F32) for p in _split3(gb_j))
            gct_j = gc_j.T
            gbc.append(gb_j)
            gc.append(gc_j)
            gct.append(gct_j)
            eg.append(jnp.exp(gc_j))
            ekd_t.append(jnp.exp(gct_j[:, CHUNK - 1:CHUNK] - gct_j))
            ecd_s[g * group + j] = jnp.exp(gc_j[CHUNK - 1:CHUNK, :])

        def rows_of(j):
            return pl.ds(pl.multiple_of((g * group + j) * CHUNK, CHUNK), CHUNK)

        q = {p: qkvs[rows_of(p[0]), p[1] * DK:(p[1] + 1) * DK] for p in probs}
        k = {p: qkvs[rows_of(p[0]), HK + p[1] * DK:HK + (p[1] + 1) * DK] for p in probs}
        v = {p: qkvs[rows_of(p[0]), 2 * HK + p[1] * DV:2 * HK + (p[1] + 1) * DV] for p in probs}
        kt = {p: k[p].T for p in probs}
        bcol = {p: gbc[p[0]][:, HEADS + p[1]:HEADS + p[1] + 1] for p in probs}
        egh = {p: eg[p[0]][:, p[1]:p[1] + 1] for p in probs}
        kb = {p: k[p] * bcol[p] for p in probs}
        ktb = {p: kt[p].astype(BF16) for p in probs}
        kk = {p: _bdot(kb[p], ktb[p]) for p in probs}
        qk = {p: _bdot(q[p], ktb[p]) for p in probs}
        decay = {p: jnp.exp(jnp.where(incl, gc[p[0]][:, p[1]:p[1] + 1] - gct[p[0]][p[1]:p[1] + 1, :],
                                      -jnp.inf)) for p in probs}
        a_kk = {p: jnp.where(strict, kk[p] * decay[p], 0.0) for p in probs}
        for p in probs:
            aqk_s[g * group + p[0], p[1]] = jnp.where(incl, qk[p] * decay[p], 0.0).astype(BF16)
            kdt_s[g * group + p[0], p[1]] = (kt[p] * ekd_t[p[0]][p[1]:p[1] + 1, :]).astype(BF16)
        t = {p: eye - a_kk[p] * masks_ref[0] for p in probs}
        for lvl in range(1, N_LEVELS):
            tb = {p: t[p].astype(BF16) for p in probs}
            x = {p: _bdot(a_kk[p] * masks_ref[lvl], tb[p]) for p in probs}
            t = {p: t[p] - _bdot(tb[p], x[p]) for p in probs}
        for p in probs:
            sol = _bdot(t[p], jnp.concatenate([v[p] * bcol[p], kb[p] * egh[p]], axis=1))
            u_s[g * group + p[0], p[1]] = sol[:, :DV]
            wq_s[g * group + p[0], p[1]] = jnp.concatenate([sol[:, DV:], q[p] * egh[p]],
                                                           axis=0).astype(BF16)
        return carry

    lax.fori_loop(0, nc // group, prep_body, 0)

    def scan_body(c, carry):
        hs = range(HEADS)
        rs = pl.ds(pl.multiple_of(c * CHUNK, CHUNK), CHUNK)
        ecd = ecd_s[c]
        s = [state[h] for h in hs]
        ws_qs = [jnp.dot(wq_s[c, h], s[h].astype(BF16), preferred_element_type=F32) for h in hs]
        v_new = [u_s[c, h] - ws_qs[h][:CHUNK] for h in hs]
        vb = [v_new[h].astype(BF16) for h in hs]
        o = [ws_qs[h][CHUNK:] + jnp.dot(aqk_s[c, h], vb[h], preferred_element_type=F32) for h in hs]
        for h in hs:
            state[h] = s[h] * ecd[:, h:h + 1] + jnp.dot(kdt_s[c, h], vb[h], preferred_element_type=F32)
        for h in hs:
            on = _rmsnorm(o[h], nw_ref[...])
            o_ref[rs, h * DV:(h + 1) * DV] = (on * zbuf[rs, h * DV:(h + 1) * DV]).astype(o_ref.dtype)
        return carry

    lax.fori_loop(0, nc, scan_body, 0)

    @pl.when(i == pl.num_programs(1) - 1)
    def _():
        sout_ref[...] = state[...]


def _ffn_kernel(h_ref, c2_ref, o2_ref, hg_in_ref, hv_in_ref, nmw_ref, wgate_ref, bgate_ref,
                wco_ref, wdo_ref, wout_ref, nfw_ref, wupg_ref, wupv_ref, dwg_ref, dwv_ref,
                dbg_ref, dbv_ref, wdown_ref, nlw_ref, out_ref, hg_out_ref, hv_out_ref,
                gbuf, vbuf, halo_g, halo_v, *, tl, head_pass):
    i = pl.program_id(1)
    h = h_ref[...]
    gates = _sigmoid(jnp.dot(_rmsnorm(h, nmw_ref[...]).astype(BF16), wgate_ref[...],
                             preferred_element_type=F32) + bgate_ref[...])
    yc = jnp.dot(c2_ref[...], wco_ref[...], preferred_element_type=F32)
    yd = jnp.dot(o2_ref[...], wdo_ref[...], preferred_element_type=F32)
    mix = gates[:, :D_MODEL] * yc + gates[:, D_MODEL:] * yd
    h1 = h + _bdot(mix, wout_ref[...])
    if head_pass:
        h1 = jnp.where(_pad_row_mask(h1.shape), h1, 0.0)
    u2 = _rmsnorm(h1, nfw_ref[...]).astype(BF16)

    @pl.when(i == 0)
    def _():
        halo_g[...] = hg_in_ref[...]
        halo_v[...] = hv_in_ref[...]

    first = FFN_HALO - (FFN_CONV_K - 1)
    acc = h1
    for blk in range(D_FF // FF_BLOCK):
        cs = slice(blk * FF_BLOCK, (blk + 1) * FF_BLOCK)
        ys = []
        for buf, halo, wup, dw, db in ((gbuf, halo_g, wupg_ref, dwg_ref, dbg_ref),
                                      (vbuf, halo_v, wupv_ref, dwv_ref, dbv_ref)):
            buf[0:FFN_HALO, :] = halo[:, cs]
            buf[FFN_HALO:FFN_HALO + tl, :] = jnp.dot(u2, wup[:, cs], preferred_element_type=F32)
            halo[:, cs] = buf[tl:tl + FFN_HALO, :]
            y = db[:, cs] + dw[0:1, cs] * buf[first:first + tl, :]
            for j in range(1, FFN_CONV_K):
                y = y + dw[j:j + 1, cs] * buf[first + j:first + j + tl, :]
            ys.append(y)
        act = (_silu(ys[0]) * ys[1]).astype(BF16)
        acc = acc + jnp.dot(act, wdown_ref[cs, :], preferred_element_type=F32)
    out_ref[...] = _rmsnorm(acc, nlw_ref[...])

    @pl.when(i == pl.num_programs(1) - 1)
    def _():
        hg_out_ref[...] = halo_g[...]
        hv_out_ref[...] = halo_v[...]


def _const_spec(shape):
    nd = len(shape)
    return pl.BlockSpec(shape, lambda b, i: (0,) * nd, pipeline_mode=pl.Buffered(1))


def _params(sem):
    return pltpu.CompilerParams(dimension_semantics=sem, vmem_limit_bytes=VMEM_LIMIT)


def _level_masks():
    r = jnp.arange(CHUNK)[:, None]
    c = jnp.arange(CHUNK)[None, :]
    masks = []
    for lvl in range(N_LEVELS):
        s = 1 << lvl
        masks.append(((r // (2 * s)) == (c // (2 * s))) & ((r // s) % 2 == 1) & ((c // s) % 2 == 0))
    return jnp.stack(masks).astype(F32)


def _layer(h, carries, wts, *, tl, head_pass):
    bsz, length, _ = h.shape
    assert length % tl == 0
    grid = (bsz, length // tl)
    conf_halo, qkv_halo, s0, ffn_halo_g, ffn_halo_v = carries
    tile = lambda width: pl.BlockSpec((None, tl, width), lambda b, i: (b, i, 0))
    cs = lambda a: _const_spec(a.shape)
    sds = jax.ShapeDtypeStruct
    tag = "head" if head_pass else "body"

    c, qkv, gb = pl.pallas_call(
        functools.partial(_inproj_kernel, head_pass=head_pass),
        grid=grid,
        in_specs=[tile(D_MODEL), cs(wts["nmw"]), cs(wts["w_c"]), cs(wts["w_qkv"]), cs(wts["w_gb"]),
                  cs(wts["alog"]), cs(wts["dtb"])],
        out_specs=[tile(DC), tile(QKV), tile(LANE)],
        out_shape=[sds((bsz, length, DC), F32), sds((bsz, length, QKV), F32),
                   sds((bsz, length, LANE), F32)],
        compiler_params=_params(("parallel", "parallel")),
        name="inproj_" + tag,
    )(h, wts["nmw"], wts["w_c"], wts["w_qkv"], wts["w_gb"], wts["alog"], wts["dtb"])

    c2 = pl.pallas_call(
        functools.partial(_conf_kernel, tl=tl),
        grid=grid,
        in_specs=[tile(DC), cs(conf_halo), cs(wts["conf_w"]), cs(wts["conf_b"]),
                  cs(wts["ln_w"]), cs(wts["ln_b"])],
        out_specs=tile(DC),
        out_shape=sds((bsz, length, DC), BF16),
        scratch_shapes=[pltpu.VMEM((tl + CONF_HALO, DC), F32), pltpu.VMEM((tl, DC), F32)],
        compiler_params=_params(("arbitrary", "arbitrary")),
        name="conf_conv_" + tag,
    )(c, conf_halo, wts["conf_w"], wts["conf_b"], wts["ln_w"], wts["ln_b"])

    nc = tl // CHUNK
    group = min(CHUNK_GROUP, nc)
    o2, s_out = pl.pallas_call(
        functools.partial(_gdn_kernel, tl=tl, group=group),
        grid=grid,
        in_specs=[tile(QKV), tile(D_MODEL), tile(LANE), cs(qkv_halo), cs(s0), cs(wts["dn_cw"]),
                  cs(wts["dn_nw"]), cs(wts["nmw"]), cs(wts["w_z"]), cs(wts["tri"]), cs(wts["masks"])],
        out_specs=[tile(HV), pl.BlockSpec((None, HEADS, DK, DV), lambda b, i: (b, 0, 0, 0))],
        out_shape=[sds((bsz, length, HV), BF16), sds((bsz, HEADS, DK, DV), F32)],
        scratch_shapes=[pltpu.VMEM((tl + QKV_HALO, QKV), F32), pltpu.VMEM((tl, QKV), F32),
                        pltpu.VMEM((tl, HV), F32),
                        pltpu.VMEM((nc, HEADS, 2 * CHUNK, DK), BF16),
                        pltpu.VMEM((nc, HEADS, CHUNK, DV), F32),
                        pltpu.VMEM((nc, HEADS, CHUNK, CHUNK), BF16),
                        pltpu.VMEM((nc, HEADS, DK, CHUNK), BF16),
                        pltpu.VMEM((nc, 1, LANE), F32),
                        pltpu.VMEM((HEADS, DK, DV), F32)],
        compiler_params=_params(("arbitrary", "arbitrary")),
        name="gdn_" + tag,
    )(qkv, h, gb, qkv_halo, s0, wts["dn_cw"], wts["dn_nw"], wts["nmw"], wts["w_z"],
      wts["tri"], wts["masks"])

    halo_spec = pl.BlockSpec((None, FFN_HALO, D_FF), lambda b, i: (b, 0, 0))
    out, hg, hv = pl.pallas_call(
        functools.partial(_ffn_kernel, tl=tl, head_pass=head_pass),
        grid=grid,
        in_specs=[tile(D_MODEL), tile(DC), tile(HV), cs(ffn_halo_g), cs(ffn_halo_v),
                  cs(wts["nmw"]), cs(wts["w_gate"]), cs(wts["b_gate"]), cs(wts["w_co"]),
                  cs(wts["w_do"]), cs(wts["w_out"]), cs(wts["nfw"]), cs(wts["w_upg"]),
                  cs(wts["w_upv"]), cs(wts["dw_g"]), cs(wts["dw_v"]), cs(wts["db_g"]),
                  cs(wts["db_v"]), cs(wts["w_down"]), cs(wts["nlw"])],
        out_specs=[tile(D_MODEL), halo_spec, halo_spec],
        out_shape=[sds((bsz, length, D_MODEL), F32), sds((bsz, FFN_HALO, D_FF), F32),
                   sds((bsz, FFN_HALO, D_FF), F32)],
        scratch_shapes=[pltpu.VMEM((tl + FFN_HALO, FF_BLOCK), F32),
                        pltpu.VMEM((tl + FFN_HALO, FF_BLOCK), F32),
                        pltpu.VMEM((FFN_HALO, D_FF), F32), pltpu.VMEM((FFN_HALO, D_FF), F32)],
        compiler_params=_params(("arbitrary", "arbitrary")),
        name="merge_ffn_" + tag,
    )(h, c2, o2, ffn_halo_g, ffn_halo_v, wts["nmw"], wts["w_gate"], wts["b_gate"], wts["w_co"],
      wts["w_do"], wts["w_out"], wts["nfw"], wts["w_upg"], wts["w_upv"], wts["dw_g"], wts["dw_v"],
      wts["db_g"], wts["db_v"], wts["w_down"], wts["nlw"])

    new_carries = (c[0, length - CONF_HALO:], qkv[0, length - QKV_HALO:], s_out[0], hg[0], hv[0])
    return out, new_carries


def kernel(x, meta_tokens, norm_mix_w, w_in, b_gate, conf_dw_w, conf_dw_b, conf_ln_w, conf_ln_b, w_conf_out, dn_conv_w, dn_A_log, dn_dt_bias, dn_norm_w, w_dn_out, w_out, norm_ffn_w, w_up, ffn_dw_w, ffn_dw_b, w_down, norm_final_w):
    bsz, seq, _ = x.shape
    assert seq % BODY_ROWS == 0, "sequence must be a whole number of body tiles"
    assert w_in.shape[0] == 1, "single-layer block"

    wi = w_in[0]
    o_q = 2 * DC
    o_z = o_q + QKV
    o_a = o_z + HV
    o_gate = o_a + 2 * HEADS
    row = lambda v: v.reshape(1, -1).astype(F32)
    lane_pad = lambda v: jnp.pad(row(v), ((0, 0), (0, LANE - v.shape[-1])))
    wu = w_up[0]
    wts = {
        "nmw": row(norm_mix_w[0]),
        "w_c": wi[:, :o_q].astype(BF16),
        "w_qkv": wi[:, o_q:o_z].astype(BF16),
        "w_z": wi[:, o_z:o_a].astype(BF16),
        "w_gb": jnp.pad(wi[:, o_a:o_gate], ((0, 0), (0, LANE - 2 * HEADS))).astype(BF16),
        "w_gate": wi[:, o_gate:].astype(BF16),
        "b_gate": row(b_gate[0]),
        "alog": lane_pad(dn_A_log[0]),
        "dtb": lane_pad(dn_dt_bias[0]),
        "conf_w": conf_dw_w[0].astype(F32),
        "conf_b": row(conf_dw_b[0]),
        "ln_w": row(conf_ln_w[0]),
        "ln_b": row(conf_ln_b[0]),
        "dn_cw": dn_conv_w[0].astype(F32),
        "dn_nw": row(dn_norm_w[0]),
        "tri": (jnp.arange(CHUNK)[:, None] >= jnp.arange(CHUNK)[None, :]).astype(BF16),
        "masks": _level_masks(),
        "w_co": w_conf_out[0].astype(BF16),
        "w_do": w_dn_out[0].astype(BF16),
        "w_out": w_out[0].astype(BF16),
        "nfw": row(norm_ffn_w[0]),
        "w_upg": wu[:, :D_FF].astype(BF16),
        "w_upv": wu[:, D_FF:].astype(BF16),
        "dw_g": ffn_dw_w[0][:, :D_FF].astype(F32),
        "dw_v": ffn_dw_w[0][:, D_FF:].astype(F32),
        "db_g": row(ffn_dw_b[0][:D_FF]),
        "db_v": row(ffn_dw_b[0][D_FF:]),
        "w_down": w_down[0].astype(BF16),
        "nlw": row(norm_final_w),
    }

    zero_carries = (jnp.zeros((CONF_HALO, DC), F32), jnp.zeros((QKV_HALO, QKV), F32),
                    jnp.zeros((HEADS, DK, DV), F32), jnp.zeros((FFN_HALO, D_FF), F32),
                    jnp.zeros((FFN_HALO, D_FF), F32))
    head = jnp.concatenate([jnp.zeros((PAD, D_MODEL), x.dtype), meta_tokens.astype(x.dtype)])[None]
    _, carries = _layer(head, zero_carries, wts, tl=CHUNK, head_pass=True)
    out, _ = _layer(x, carries, wts, tl=BODY_ROWS, head_pass=False)
    return out
```

```python
import functools

import jax
import jax.numpy as jnp
from jax import lax
from jax.experimental import pallas as pl
from jax.experimental.pallas import tpu as pltpu

D_MODEL = 1024
N_META = 16
CHUNK = 64
PAD = CHUNK - N_META
DC = D_MODEL
CONF_K = 31
HEADS = 8
DK = 128
DV = 128
HK = HEADS * DK
HV = HEADS * DV
QKV = 2 * HK + HV
DN_CONV_K = 4
D_FF = 2816
FFN_CONV_K = 3
EPS = 1e-6

LANE = 128
SUBLANE = 8
BODY_ROWS = 512
MXU_DIM = 256
FF_BLOCKS = (6 * MXU_DIM, 5 * MXU_DIM)
QKV_BLOCK = 2 * MXU_DIM
CHUNK_GROUP = 8
VMEM_LIMIT = 56 * 1024 * 1024

CONF_HALO = 32
QKV_HALO = SUBLANE
FFN_HALO = SUBLANE
N_LEVELS = 6
STATE_SHAPE = (HEADS // 2, DK, 2 * DV)

F32 = jnp.float32
BF16 = jnp.bfloat16


def _bdot(a, b):
    return jnp.dot(a.astype(BF16), b.astype(BF16), preferred_element_type=F32)


def _sigmoid(x):
    return 1.0 / (1.0 + jnp.exp(-x))


def _silu(x):
    return x * _sigmoid(x)


def _softplus(x):
    return jnp.maximum(x, 0.0) + jnp.log(1.0 + jnp.exp(-jnp.abs(x)))


def _rmsnorm(x, w):
    return x * lax.rsqrt(jnp.mean(x * x, axis=-1, keepdims=True) + EPS) * w


def _split3(x):
    hi = x.astype(BF16)
    r = x - hi.astype(F32)
    mid = r.astype(BF16)
    lo = (r - mid.astype(F32)).astype(BF16)
    return hi, mid, lo


def _pad_row_mask(shape):
    return lax.broadcasted_iota(jnp.int32, shape, 0) >= PAD


def _inproj_kernel(h_ref, halo_ref, nw_ref, wc_ref, wqkv_ref, wgb_ref, alog_ref, dtb_ref, cw_ref,
                   c_ref, qkv_ref, gb_ref, tail_ref, *stages, tl, head_pass):
    i = pl.program_id(1)
    nblk = QKV // QKV_BLOCK
    u = _rmsnorm(h_ref[...], nw_ref[...]).astype(BF16)

    @pl.when(i == 0)
    def _():
        for cb in range(nblk):
            stages[cb][0:QKV_HALO, :] = halo_ref[:, cb * QKV_BLOCK:(cb + 1) * QKV_BLOCK]

    @pl.when(i > 0)
    def _():
        for cb in range(nblk):
            stages[cb][0:QKV_HALO, :] = stages[cb][tl:tl + QKV_HALO, :]

    def project(cb):
        stages[cb][QKV_HALO:QKV_HALO + tl, :] = jnp.dot(
            u, wqkv_ref[:, cb * QKV_BLOCK:(cb + 1) * QKV_BLOCK], preferred_element_type=F32)

    def conv_act(cb):
        first = QKV_HALO - (DN_CONV_K - 1)
        rows = CHUNK
        for lbb in range(QKV_BLOCK // LANE):
            lb = cb * (QKV_BLOCK // LANE) + lbb
            ls = slice(lb * LANE, (lb + 1) * LANE)
            bs = slice(lbb * LANE, (lbb + 1) * LANE)
            for rb in range(tl // rows):
                r0 = first + rb * rows
                acc = cw_ref[0:1, ls] * stages[cb][r0:r0 + rows, bs]
                for j in range(1, DN_CONV_K):
                    acc = acc + cw_ref[j:j + 1, ls] * stages[cb][r0 + j:r0 + j + rows, bs]
                y = _silu(acc)
                if lb < 2 * HEADS:
                    y = y * lax.rsqrt(jnp.sum(y * y, axis=-1, keepdims=True) + EPS)
                if lb < HEADS:
                    y = y * (DK ** -0.5)
                qkv_ref[rb * rows:(rb + 1) * rows, ls] = y

    project(0)
    for cb in range(nblk):
        if cb + 1 < nblk:
            project(cb + 1)
        else:
            cin = jnp.dot(u, wc_ref[...], preferred_element_type=F32)
            c_ref[...] = cin[:, :DC] * _sigmoid(cin[:, DC:])
        conv_act(cb)

    p = jnp.dot(u, wgb_ref[...], preferred_element_type=F32)
    g = -jnp.exp(alog_ref[...]) * _softplus(p + dtb_ref[...])
    lane = lax.broadcasted_iota(jnp.int32, p.shape, 1)
    val = jnp.where(lane < HEADS, g, jnp.where(lane < 2 * HEADS, _sigmoid(p), 0.0))
    if head_pass:
        val = jnp.where(_pad_row_mask(p.shape), val, 0.0)
    gb_ref[...] = val

    @pl.when(i == pl.num_programs(1) - 1)
    def _():
        for cb in range(nblk):
            tail_ref[:, cb * QKV_BLOCK:(cb + 1) * QKV_BLOCK] = stages[cb][tl:tl + QKV_HALO, :]


def _conf_kernel(c_ref, halo_ref, w_ref, b_ref, lnw_ref, lnb_ref, o_ref, cbuf, ybuf, *, tl):
    i = pl.program_id(1)

    @pl.when(i == 0)
    def _():
        cbuf[0:CONF_HALO, :] = halo_ref[...]

    @pl.when(i > 0)
    def _():
        cbuf[0:CONF_HALO, :] = cbuf[tl:tl + CONF_HALO, :]

    cbuf[CONF_HALO:CONF_HALO + tl, :] = c_ref[...]
    first = CONF_HALO - (CONF_K - 1)
    rows = CHUNK
    for lb in range(DC // LANE):
        ls = slice(lb * LANE, (lb + 1) * LANE)
        for rb in range(tl // rows):
            base = rb * rows
            y = jnp.broadcast_to(b_ref[:, ls], (rows, LANE))
            for r in range(SUBLANE):
                span = rows + (SUBLANE if r else 0)
                z = None
                for j in range(CONF_K):
                    if (first + j) % SUBLANE != r:
                        continue
                    a = base + first + j - r
                    term = w_ref[j:j + 1, ls] * cbuf[a:a + span, ls]
                    z = term if z is None else z + term
                if z is not None:
                    y = y + z[r:r + rows]
            ybuf[base:base + rows, ls] = y
    y = ybuf[...]
    mu = jnp.mean(y, axis=-1, keepdims=True)
    yc = y - mu
    yn = yc * lax.rsqrt(jnp.mean(yc * yc, axis=-1, keepdims=True) + EPS)
    o_ref[...] = _silu(yn * lnw_ref[...] + lnb_ref[...]).astype(o_ref.dtype)


def _gdn_kernel(qkvs, h_ref, gb_ref, s0_ref, nw_ref, nmw_ref, wz_ref, tri_ref, masks_ref,
                o_ref, sout_ref, zbuf, wq_s, u_s, aqk_s, kdt_s, ecd_s, state, *, tl, group):
    i = pl.program_id(1)
    nc = tl // CHUNK

    @pl.when(i == 0)
    def _():
        state[...] = s0_ref[...]

    zbuf[...] = _silu(jnp.dot(_rmsnorm(h_ref[...], nmw_ref[...]).astype(BF16), wz_ref[...],
                              preferred_element_type=F32))

    lane = lax.broadcasted_iota(jnp.int32, (CHUNK, LANE), 1)
    ri = lax.broadcasted_iota(jnp.int32, (CHUNK, LANE), 0)
    ci = jnp.bitwise_and(lane, CHUNK - 1)
    lo = lane < CHUNK
    lo2 = lax.broadcasted_iota(jnp.int32, (2 * CHUNK, LANE), 1) < CHUNK

    def block_diag(m):
        keep = lo if m.shape[0] == CHUNK else lo2
        zero = jnp.zeros_like(m)
        return jnp.concatenate([jnp.where(keep, m, zero), jnp.where(keep, zero, m)], axis=0)

    def block_diag_wide(a, b):
        za = jnp.zeros_like(a)
        return jnp.concatenate([jnp.concatenate([a, za], axis=1), jnp.concatenate([za, b], axis=1)],
                               axis=0)

    def lane_pair(col0, col1):
        r = col0.shape[0]
        return jnp.concatenate([jnp.broadcast_to(col0, (r, LANE)), jnp.broadcast_to(col1, (r, LANE))],
                               axis=1)

    def prep_body(g, carry):
        probs = [(j, p) for j in range(group) for p in range(HEADS // 2)]
        incl = ri >= ci
        strict = ri > ci
        eye = jnp.where(ri == ci, 1.0, 0.0)
        tri = tri_ref[...]
        gbc, gc, gst, eg, ekd_t = [], [], [], [], []
        for j in range(group):
            rs = pl.ds(pl.multiple_of((g * group + j) * CHUNK, CHUNK), CHUNK)
            gb_j = gb_ref[rs, :]
            gc_j = sum(jnp.dot(tri, p, preferred_element_type=F32) for p in _split3(gb_j))
            gst_j = jnp.concatenate([gc_j, pltpu.roll(gc_j, LANE - 1, axis=1)], axis=0).T
            glast = jnp.where(lo2, gst_j[:, CHUNK - 1:CHUNK], gst_j[:, LANE - 1:LANE])
            gbc.append(gb_j)
            gc.append(gc_j)
            gst.append(gst_j)
            eg.append(jnp.exp(gc_j))
            ekd_t.append(jnp.exp(glast - gst_j))
            ecd_s[g * group + j] = jnp.exp(gc_j[CHUNK - 1:CHUNK, :])

        def rows_of(j):
            return pl.ds(pl.multiple_of((g * group + j) * CHUNK, CHUNK), CHUNK)

        def cols_of(p, base):
            return slice(base + 2 * p * DK, base + (2 * p + 2) * DK)

        q2 = {pr: qkvs[rows_of(pr[0]), cols_of(pr[1], 0)] for pr in probs}
        k2 = {pr: qkvs[rows_of(pr[0]), cols_of(pr[1], HK)] for pr in probs}
        v2 = {pr: qkvs[rows_of(pr[0]), cols_of(pr[1], 2 * HK)] for pr in probs}
        k2t = {pr: jnp.concatenate([k2[pr][:, :DK], k2[pr][:, DK:]], axis=0).T for pr in probs}
        b2 = {pr: lane_pair(gbc[pr[0]][:, HEADS + 2 * pr[1]:HEADS + 2 * pr[1] + 1],
                            gbc[pr[0]][:, HEADS + 2 * pr[1] + 1:HEADS + 2 * pr[1] + 2]) for pr in probs}
        eg2 = {pr: lane_pair(eg[pr[0]][:, 2 * pr[1]:2 * pr[1] + 1],
                             eg[pr[0]][:, 2 * pr[1] + 1:2 * pr[1] + 2]) for pr in probs}
        kb2 = {pr: k2[pr] * b2[pr] for pr in probs}
        kq = {pr: _bdot(jnp.concatenate([kb2[pr], q2[pr]], axis=0), block_diag(k2t[pr].astype(BF16)))
              for pr in probs}
        decay = {}
        for pr in probs:
            j, p = pr
            gcol = jnp.where(lo, gc[j][:, 2 * p:2 * p + 1], gc[j][:, 2 * p + 1:2 * p + 2])
            decay[pr] = jnp.exp(jnp.where(incl, gcol - gst[j][2 * p:2 * p + 1, :], -jnp.inf))
        a_kk = {pr: jnp.where(strict, kq[pr][:CHUNK] * decay[pr], 0.0) for pr in probs}
        for pr in probs:
            j, p = pr
            aqk_s[g * group + j, p] = jnp.where(incl, kq[pr][CHUNK:] * decay[pr], 0.0).astype(BF16)
            kdt_s[g * group + j, p] = (k2t[pr] * ekd_t[j][2 * p:2 * p + 1, :]).astype(BF16)
        t = {pr: eye - a_kk[pr] * masks_ref[0] for pr in probs}
        for lvl in range(1, N_LEVELS):
            tb = {pr: t[pr].astype(BF16) for pr in probs}
            x = {pr: jnp.dot((a_kk[pr] * masks_ref[lvl]).astype(BF16), block_diag(tb[pr]),
                             preferred_element_type=F32) for pr in probs}
            t = {pr: t[pr] - jnp.dot(tb[pr], block_diag(x[pr].astype(BF16)),
                                     preferred_element_type=F32) for pr in probs}
        for pr in probs:
            j, p = pr
            rv = (v2[pr] * b2[pr]).astype(BF16)
            rk = (kb2[pr] * eg2[pr]).astype(BF16)
            rhs = block_diag_wide(jnp.concatenate([rv[:, :DV], rk[:, :DK]], axis=1),
                                  jnp.concatenate([rv[:, DV:], rk[:, DK:]], axis=1))
            sol = jnp.dot(t[pr].astype(BF16), rhs, preferred_element_type=F32)
            u_s[g * group + j, p] = jnp.concatenate([sol[:, :DV], sol[:, DV + DK:2 * DV + DK]], axis=1)
            w2 = jnp.concatenate([sol[:, DV:DV + DK], sol[:, 2 * DV + DK:]], axis=1)
            wq_s[g * group + j, p] = jnp.concatenate([w2, q2[pr] * eg2[pr]], axis=0).astype(BF16)
        return carry

    lax.fori_loop(0, nc // group, prep_body, 0)

    def scan_body(c, carry):
        ps = range(HEADS // 2)
        rs = pl.ds(c * CHUNK, CHUNK)
        ecd = ecd_s[c]
        s = [state[p] for p in ps]
        sb = [s[p].astype(BF16) for p in ps]
        ws_qs = [jnp.dot(wq_s[c, p], block_diag_wide(sb[p][:, :DV], sb[p][:, DV:]),
                         preferred_element_type=F32) for p in ps]
        vb = [(u_s[c, p] - ws_qs[p][:CHUNK]).astype(BF16) for p in ps]
        bdv = [block_diag_wide(vb[p][:, :DV], vb[p][:, DV:]) for p in ps]
        o = [ws_qs[p][CHUNK:] + jnp.dot(aqk_s[c, p], bdv[p], preferred_element_type=F32) for p in ps]
        for p in ps:
            cd = lane_pair(ecd[:, 2 * p:2 * p + 1], ecd[:, 2 * p + 1:2 * p + 2])
            state[p] = s[p] * cd + jnp.dot(kdt_s[c, p], bdv[p], preferred_element_type=F32)
        for p in ps:
            for half in range(2):
                h = 2 * p + half
                on = _rmsnorm(o[p][:, half * DV:(half + 1) * DV], nw_ref[...])
                o_ref[rs, h * DV:(h + 1) * DV] = (on * zbuf[rs, h * DV:(h + 1) * DV]).astype(o_ref.dtype)
        return carry

    for c in range(nc):
        scan_body(c, 0)

    @pl.when(i == pl.num_programs(1) - 1)
    def _():
        sout_ref[...] = state[...]


def _ffn_kernel(h_ref, c2_ref, o2_ref, hg_in_ref, hv_in_ref, nmw_ref, wgate_ref, bgate_ref,
                wco_ref, wdo_ref, wout_ref, nfw_ref, wupg_ref, wupv_ref, dwg_ref, dwv_ref,
                dbg_ref, dbv_ref, wdown_ref, nlw_ref, out_ref, hg_out_ref, hv_out_ref,
                gbuf, vbuf, halo_g, halo_v, *, tl, head_pass):
    i = pl.program_id(1)
    h = h_ref[...]
    gates = _sigmoid(jnp.dot(_rmsnorm(h, nmw_ref[...]).astype(BF16), wgate_ref[...],
                             preferred_element_type=F32) + bgate_ref[...])
    yc = jnp.dot(c2_ref[...], wco_ref[...], preferred_element_type=F32)
    yd = jnp.dot(o2_ref[...], wdo_ref[...], preferred_element_type=F32)
    mix = gates[:, :D_MODEL] * yc + gates[:, D_MODEL:] * yd
    h1 = h + _bdot(mix, wout_ref[...])
    if head_pass:
        h1 = jnp.where(_pad_row_mask(h1.shape), h1, 0.0)
    u2 = _rmsnorm(h1, nfw_ref[...]).astype(BF16)

    @pl.when(i == 0)
    def _():
        halo_g[...] = hg_in_ref[...]
        halo_v[...] = hv_in_ref[...]

    first = FFN_HALO - (FFN_CONV_K - 1)
    acc = h1
    start = 0
    for width in FF_BLOCKS:
        cs = slice(start, start + width)
        start += width
        ys = []
        for buf, halo, wup, dw, db in ((gbuf, halo_g, wupg_ref, dwg_ref, dbg_ref),
                                      (vbuf, halo_v, wupv_ref, dwv_ref, dbv_ref)):
            buf[0:FFN_HALO, 0:width] = halo[:, cs]
            buf[FFN_HALO:FFN_HALO + tl, 0:width] = jnp.dot(u2, wup[:, cs], preferred_element_type=F32)
            halo[:, cs] = buf[tl:tl + FFN_HALO, 0:width]
            y = db[:, cs] + dw[0:1, cs] * buf[first:first + tl, 0:width]
            for j in range(1, FFN_CONV_K):
                y = y + dw[j:j + 1, cs] * buf[first + j:first + j + tl, 0:width]
            ys.append(y)
        act = (_silu(ys[0]) * ys[1]).astype(BF16)
        acc = acc + jnp.dot(act, wdown_ref[cs, :], preferred_element_type=F32)
    out_ref[...] = _rmsnorm(acc, nlw_ref[...])

    @pl.when(i == pl.num_programs(1) - 1)
    def _():
        hg_out_ref[...] = halo_g[...]
        hv_out_ref[...] = halo_v[...]


def _const_spec(shape):
    nd = len(shape)
    return pl.BlockSpec(shape, lambda b, i: (0,) * nd, pipeline_mode=pl.Buffered(1))


def _params(sem):
    return pltpu.CompilerParams(dimension_semantics=sem, vmem_limit_bytes=VMEM_LIMIT)


def _level_masks():
    r = jnp.arange(CHUNK)[:, None]
    c = jnp.arange(CHUNK)[None, :]
    masks = []
    for lvl in range(N_LEVELS):
        s = 1 << lvl
        masks.append(((r // (2 * s)) == (c // (2 * s))) & ((r // s) % 2 == 1) & ((c // s) % 2 == 0))
    return jnp.tile(jnp.stack(masks).astype(F32), (1, 1, LANE // CHUNK))


def _layer(h, carries, wts, *, tl, head_pass):
    bsz, length, _ = h.shape
    assert length % tl == 0
    grid = (bsz, length // tl)
    conf_halo, qkv_halo, s0, ffn_halo_g, ffn_halo_v = carries
    tile = lambda width: pl.BlockSpec((None, tl, width), lambda b, i: (b, i, 0))
    cs = lambda a: _const_spec(a.shape)
    sds = jax.ShapeDtypeStruct
    tag = "head" if head_pass else "body"

    c, qkv, gb, qkv_tail = pl.pallas_call(
        functools.partial(_inproj_kernel, tl=tl, head_pass=head_pass),
        grid=grid,
        in_specs=[tile(D_MODEL), cs(qkv_halo), cs(wts["nmw"]), cs(wts["w_c"]), cs(wts["w_qkv"]),
                  cs(wts["w_gb"]), cs(wts["alog"]), cs(wts["dtb"]), cs(wts["dn_cw"])],
        out_specs=[tile(DC), tile(QKV), tile(LANE),
                   pl.BlockSpec((None, QKV_HALO, QKV), lambda b, i: (b, 0, 0))],
        out_shape=[sds((bsz, length, DC), F32), sds((bsz, length, QKV), F32),
                   sds((bsz, length, LANE), F32), sds((bsz, QKV_HALO, QKV), F32)],
        scratch_shapes=[pltpu.VMEM((tl + QKV_HALO, QKV_BLOCK), F32)] * (QKV // QKV_BLOCK),
        compiler_params=_params(("arbitrary", "arbitrary")),
        name="inproj_" + tag,
    )(h, qkv_halo, wts["nmw"], wts["w_c"], wts["w_qkv"], wts["w_gb"], wts["alog"], wts["dtb"],
      wts["dn_cw"])

    c2 = pl.pallas_call(
        functools.partial(_conf_kernel, tl=tl),
        grid=grid,
        in_specs=[tile(DC), cs(conf_halo), cs(wts["conf_w"]), cs(wts["conf_b"]),
                  cs(wts["ln_w"]), cs(wts["ln_b"])],
        out_specs=tile(DC),
        out_shape=sds((bsz, length, DC), BF16),
        scratch_shapes=[pltpu.VMEM((tl + CONF_HALO, DC), F32), pltpu.VMEM((tl, DC), F32)],
        compiler_params=_params(("arbitrary", "arbitrary")),
        name="conf_conv_" + tag,
    )(c, conf_halo, wts["conf_w"], wts["conf_b"], wts["ln_w"], wts["ln_b"])

    nc = tl // CHUNK
    group = min(CHUNK_GROUP, nc)
    o2, s_out = pl.pallas_call(
        functools.partial(_gdn_kernel, tl=tl, group=group),
        grid=grid,
        in_specs=[tile(QKV), tile(D_MODEL), tile(LANE), cs(s0),
                  cs(wts["dn_nw"]), cs(wts["nmw"]), cs(wts["w_z"]), cs(wts["tri"]), cs(wts["masks"])],
        out_specs=[tile(HV), pl.BlockSpec((None,) + STATE_SHAPE, lambda b, i: (b, 0, 0, 0))],
        out_shape=[sds((bsz, length, HV), BF16), sds((bsz,) + STATE_SHAPE, F32)],
        scratch_shapes=[pltpu.VMEM((tl, HV), F32),
                        pltpu.VMEM((nc, HEADS // 2, 2 * CHUNK, 2 * DK), BF16),
                        pltpu.VMEM((nc, HEADS // 2, CHUNK, 2 * DV), F32),
                        pltpu.VMEM((nc, HEADS // 2, CHUNK, LANE), BF16),
                        pltpu.VMEM((nc, HEADS // 2, DK, LANE), BF16),
                        pltpu.VMEM((nc, 1, LANE), F32),
                        pltpu.VMEM(STATE_SHAPE, F32)],
        compiler_params=_params(("arbitrary", "arbitrary")),
        name="gdn_" + tag,
    )(qkv, h, gb, s0, wts["dn_nw"], wts["nmw"], wts["w_z"], wts["tri"], wts["masks"])

    halo_spec = pl.BlockSpec((None, FFN_HALO, D_FF), lambda b, i: (b, 0, 0))
    out, hg, hv = pl.pallas_call(
        functools.partial(_ffn_kernel, tl=tl, head_pass=head_pass),
        grid=grid,
        in_specs=[tile(D_MODEL), tile(DC), tile(HV), cs(ffn_halo_g), cs(ffn_halo_v),
                  cs(wts["nmw"]), cs(wts["w_gate"]), cs(wts["b_gate"]), cs(wts["w_co"]),
                  cs(wts["w_do"]), cs(wts["w_out"]), cs(wts["nfw"]), cs(wts["w_upg"]),
                  cs(wts["w_upv"]), cs(wts["dw_g"]), cs(wts["dw_v"]), cs(wts["db_g"]),
                  cs(wts["db_v"]), cs(wts["w_down"]), cs(wts["nlw"])],
        out_specs=[tile(D_MODEL), halo_spec, halo_spec],
        out_shape=[sds((bsz, length, D_MODEL), F32), sds((bsz, FFN_HALO, D_FF), F32),
                   sds((bsz, FFN_HALO, D_FF), F32)],
        scratch_shapes=[pltpu.VMEM((tl + FFN_HALO, max(FF_BLOCKS)), F32),
                        pltpu.VMEM((tl + FFN_HALO, max(FF_BLOCKS)), F32),
                        pltpu.VMEM((FFN_HALO, D_FF), F32), pltpu.VMEM((FFN_HALO, D_FF), F32)],
        compiler_params=_params(("arbitrary", "arbitrary")),
        name="merge_ffn_" + tag,
    )(h, c2, o2, ffn_halo_g, ffn_halo_v, wts["nmw"], wts["w_gate"], wts["b_gate"], wts["w_co"],
      wts["w_do"], wts["w_out"], wts["nfw"], wts["w_upg"], wts["w_upv"], wts["dw_g"], wts["dw_v"],
      wts["db_g"], wts["db_v"], wts["w_down"], wts["nlw"])

    new_carries = (c[0, length - CONF_HALO:], qkv_tail[0], s_out[0], hg[0], hv[0])
    return out, new_carries


def kernel(x, meta_tokens, norm_mix_w, w_in, b_gate, conf_dw_w, conf_dw_b, conf_ln_w, conf_ln_b, w_conf_out, dn_conv_w, dn_A_log, dn_dt_bias, dn_norm_w, w_dn_out, w_out, norm_ffn_w, w_up, ffn_dw_w, ffn_dw_b, w_down, norm_final_w):
    bsz, seq, _ = x.shape
    assert seq % BODY_ROWS == 0, "sequence must be a whole number of body tiles"
    assert w_in.shape[0] == 1, "single-layer block"

    wi = w_in[0]
    o_q = 2 * DC
    o_z = o_q + QKV
    o_a = o_z + HV
    o_gate = o_a + 2 * HEADS
    row = lambda v: v.reshape(1, -1).astype(F32)
    lane_pad = lambda v: jnp.pad(row(v), ((0, 0), (0, LANE - v.shape[-1])))
    wu = w_up[0]
    wts = {
        "nmw": row(norm_mix_w[0]),
        "w_c": wi[:, :o_q].astype(BF16),
        "w_qkv": wi[:, o_q:o_z].astype(BF16),
        "w_z": wi[:, o_z:o_a].astype(BF16),
        "w_gb": jnp.pad(wi[:, o_a:o_gate], ((0, 0), (0, LANE - 2 * HEADS))).astype(BF16),
        "w_gate": wi[:, o_gate:].astype(BF16),
        "b_gate": row(b_gate[0]),
        "alog": lane_pad(dn_A_log[0]),
        "dtb": lane_pad(dn_dt_bias[0]),
        "conf_w": conf_dw_w[0].astype(F32),
        "conf_b": row(conf_dw_b[0]),
        "ln_w": row(conf_ln_w[0]),
        "ln_b": row(conf_ln_b[0]),
        "dn_cw": dn_conv_w[0].astype(F32),
        "dn_nw": row(dn_norm_w[0]),
        "tri": (jnp.arange(CHUNK)[:, None] >= jnp.arange(CHUNK)[None, :]).astype(BF16),
        "masks": _level_masks(),
        "w_co": w_conf_out[0].astype(BF16),
        "w_do": w_dn_out[0].astype(BF16),
        "w_out": w_out[0].astype(BF16),
        "nfw": row(norm_ffn_w[0]),
        "w_upg": wu[:, :D_FF].astype(BF16),
        "w_upv": wu[:, D_FF:].astype(BF16),
        "dw_g": ffn_dw_w[0][:, :D_FF].astype(F32),
        "dw_v": ffn_dw_w[0][:, D_FF:].astype(F32),
        "db_g": row(ffn_dw_b[0][:D_FF]),
        "db_v": row(ffn_dw_b[0][D_FF:]),
        "w_down": w_down[0].astype(BF16),
        "nlw": row(norm_final_w),
    }

    zero_carries = (jnp.zeros((CONF_HALO, DC), F32), jnp.zeros((QKV_HALO, QKV), F32),
                    jnp.zeros(STATE_SHAPE, F32), jnp.zeros((FFN_HALO, D_FF), F32),
                    jnp.zeros((FFN_HALO, D_FF), F32))
    head = jnp.concatenate([jnp.zeros((PAD, D_MODEL), x.dtype), meta_tokens.astype(x.dtype)])[None]
    _, carries = _layer(head, zero_carries, wts, tl=CHUNK, head_pass=True)
    out, _ = _layer(x, carries, wts, tl=BODY_ROWS, head_pass=False)
    return out
```

```python
import functools

import jax
import jax.numpy as jnp
from jax import lax
from jax.experimental import pallas as pl
from jax.experimental.pallas import tpu as pltpu

D_MODEL = 1024
N_META = 16
CHUNK = 64
PAD = CHUNK - N_META
DC = D_MODEL
CONF_K = 31
HEADS = 8
DK = 128
DV = 128
HK = HEADS * DK
HV = HEADS * DV
QKV = 2 * HK + HV
DN_CONV_K = 4
D_FF = 2816
FFN_CONV_K = 3
EPS = 1e-6

LANE = 128
SUBLANE = 8
BODY_ROWS = 512
MXU_DIM = 256
FF_BLOCKS = (3 * MXU_DIM, 3 * MXU_DIM, 3 * MXU_DIM, 2 * MXU_DIM)
QKV_BLOCK = 2 * MXU_DIM
CHUNK_GROUP = 8
VMEM_LIMIT = 56 * 1024 * 1024

CONF_HALO = 32
QKV_HALO = SUBLANE
FFN_HALO = SUBLANE
N_LEVELS = 6
STATE_SHAPE = (HEADS // 2, DK, 2 * DV)

F32 = jnp.float32
BF16 = jnp.bfloat16

assert sum(FF_BLOCKS) == D_FF


def _bdot(a, b):
    return jnp.dot(a.astype(BF16), b.astype(BF16), preferred_element_type=F32)


NEG_LOG2_E = -1.4426950408889634


def _sigmoid(x):
    return 1.0 / (1.0 + jnp.exp2(x * NEG_LOG2_E))


def _silu(x):
    return x * _sigmoid(x)


def _softplus(x):
    return jnp.maximum(x, 0.0) + jnp.log(1.0 + jnp.exp(-jnp.abs(x)))


def _rmsnorm(x, w):
    return x * lax.rsqrt(jnp.mean(x * x, axis=-1, keepdims=True) + EPS) * w


def _split3(x):
    hi = x.astype(BF16)
    r = x - hi.astype(F32)
    mid = r.astype(BF16)
    lo = (r - mid.astype(F32)).astype(BF16)
    return hi, mid, lo


def _pad_row_mask(shape):
    return lax.broadcasted_iota(jnp.int32, shape, 0) >= PAD


def _inproj_kernel(h_ref, halo_ref, nw_ref, wc_ref, wqkv_ref, wgb_ref, alog_ref, dtb_ref, cw_ref,
                   c_ref, qkv_ref, gb_ref, tail_ref, *stages, tl, head_pass):
    i = pl.program_id(1)
    nblk = QKV // QKV_BLOCK
    u = _rmsnorm(h_ref[...], nw_ref[...]).astype(BF16)

    @pl.when(i == 0)
    def _():
        for cb in range(nblk):
            stages[cb][0:QKV_HALO, :] = halo_ref[:, cb * QKV_BLOCK:(cb + 1) * QKV_BLOCK]

    @pl.when(i > 0)
    def _():
        for cb in range(nblk):
            stages[cb][0:QKV_HALO, :] = stages[cb][tl:tl + QKV_HALO, :]

    def project(cb):
        stages[cb][QKV_HALO:QKV_HALO + tl, :] = jnp.dot(
            u, wqkv_ref[:, cb * QKV_BLOCK:(cb + 1) * QKV_BLOCK], preferred_element_type=F32)

    def conv_act(cb):
        first = QKV_HALO - (DN_CONV_K - 1)
        rows = CHUNK
        for lbb in range(QKV_BLOCK // LANE):
            lb = cb * (QKV_BLOCK // LANE) + lbb
            ls = slice(lb * LANE, (lb + 1) * LANE)
            bs = slice(lbb * LANE, (lbb + 1) * LANE)
            for rb in range(tl // rows):
                r0 = first + rb * rows
                acc = cw_ref[0:1, ls] * stages[cb][r0:r0 + rows, bs]
                for j in range(1, DN_CONV_K):
                    acc = acc + cw_ref[j:j + 1, ls] * stages[cb][r0 + j:r0 + j + rows, bs]
                y = _silu(acc)
                if lb < 2 * HEADS:
                    inv = lax.rsqrt(jnp.sum(y * y, axis=-1, keepdims=True) + EPS)
                    y = y * (inv * (DK ** -0.5) if lb < HEADS else inv)
                qkv_ref[rb * rows:(rb + 1) * rows, ls] = y

    project(0)
    for cb in range(nblk):
        if cb + 1 < nblk:
            project(cb + 1)
        else:
            cin = jnp.dot(u, wc_ref[...], preferred_element_type=F32)
            c_ref[...] = cin[:, :DC] * _sigmoid(cin[:, DC:])
        conv_act(cb)

    p = jnp.dot(u, wgb_ref[...], preferred_element_type=F32)
    g = -jnp.exp(alog_ref[...]) * _softplus(p + dtb_ref[...])
    lane = lax.broadcasted_iota(jnp.int32, p.shape, 1)
    val = jnp.where(lane < HEADS, g, jnp.where(lane < 2 * HEADS, _sigmoid(p), 0.0))
    if head_pass:
        val = jnp.where(_pad_row_mask(p.shape), val, 0.0)
    gb_ref[...] = val

    @pl.when(i == pl.num_programs(1) - 1)
    def _():
        for cb in range(nblk):
            tail_ref[:, cb * QKV_BLOCK:(cb + 1) * QKV_BLOCK] = stages[cb][tl:tl + QKV_HALO, :]


def _conf_kernel(c_ref, halo_ref, w_ref, b_ref, lnw_ref, lnb_ref, o_ref, cbuf, ybuf, *, tl):
    i = pl.program_id(1)

    @pl.when(i == 0)
    def _():
        cbuf[0:CONF_HALO, :] = halo_ref[...]

    @pl.when(i > 0)
    def _():
        cbuf[0:CONF_HALO, :] = cbuf[tl:tl + CONF_HALO, :]

    cbuf[CONF_HALO:CONF_HALO + tl, :] = c_ref[...]
    first = CONF_HALO - (CONF_K - 1)
    rows = CHUNK
    for lb in range(DC // LANE):
        ls = slice(lb * LANE, (lb + 1) * LANE)
        for rb in range(tl // rows):
            base = rb * rows
            y = jnp.broadcast_to(b_ref[:, ls], (rows, LANE))
            for r in range(SUBLANE):
                span = rows + (SUBLANE if r else 0)
                z = None
                for j in range(CONF_K):
                    if (first + j) % SUBLANE != r:
                        continue
                    a = base + first + j - r
                    term = w_ref[j:j + 1, ls] * cbuf[a:a + span, ls]
                    z = term if z is None else z + term
                if z is not None:
                    y = y + z[r:r + rows]
            ybuf[base:base + rows, ls] = y
    y = ybuf[...]
    mu = jnp.mean(y, axis=-1, keepdims=True)
    yc = y - mu
    yn = yc * lax.rsqrt(jnp.mean(yc * yc, axis=-1, keepdims=True) + EPS)
    o_ref[...] = _silu(yn * lnw_ref[...] + lnb_ref[...]).astype(o_ref.dtype)


def _gdn_kernel(qkvs, h_ref, gb_ref, s0_ref, nw_ref, nmw_ref, wz_ref, tri_ref, masks_ref,
                o_ref, sout_ref, zbuf, wq_s, u_s, aqk_s, kdt_s, ecd_s, state, *, tl, group):
    i = pl.program_id(1)
    nc = tl // CHUNK

    @pl.when(i == 0)
    def _():
        state[...] = s0_ref[...]

    zbuf[...] = _silu(jnp.dot(_rmsnorm(h_ref[...], nmw_ref[...]).astype(BF16), wz_ref[...],
                              preferred_element_type=F32))

    lane = lax.broadcasted_iota(jnp.int32, (CHUNK, LANE), 1)
    ri = lax.broadcasted_iota(jnp.int32, (CHUNK, LANE), 0)
    ci = jnp.bitwise_and(lane, CHUNK - 1)
    lo = lane < CHUNK
    lo2 = lax.broadcasted_iota(jnp.int32, (2 * CHUNK, LANE), 1) < CHUNK

    def block_diag(m):
        keep = lo if m.shape[0] == CHUNK else lo2
        zero = jnp.zeros_like(m)
        return jnp.concatenate([jnp.where(keep, m, zero), jnp.where(keep, zero, m)], axis=0)

    def block_diag_wide(a, b):
        za = jnp.zeros_like(a)
        return jnp.concatenate([jnp.concatenate([a, za], axis=1), jnp.concatenate([za, b], axis=1)],
                               axis=0)

    def lane_pair(col0, col1):
        r = col0.shape[0]
        return jnp.concatenate([jnp.broadcast_to(col0, (r, LANE)), jnp.broadcast_to(col1, (r, LANE))],
                               axis=1)

    def prep_body(g, carry):
        probs = [(j, p) for j in range(group) for p in range(HEADS // 2)]
        incl = ri >= ci
        strict = ri > ci
        eye = jnp.where(ri == ci, 1.0, 0.0)
        tri = tri_ref[...]
        gbc, gc, gst, eg, ekd_t = [], [], [], [], []
        for j in range(group):
            rs = pl.ds(pl.multiple_of((g * group + j) * CHUNK, CHUNK), CHUNK)
            gb_j = gb_ref[rs, :]
            gc_j = sum(jnp.dot(tri, p, preferred_element_type=F32) for p in _split3(gb_j))
            gst_j = jnp.concatenate([gc_j, pltpu.roll(gc_j, LANE - 1, axis=1)], axis=0).T
            glast = jnp.where(lo2, gst_j[:, CHUNK - 1:CHUNK], gst_j[:, LANE - 1:LANE])
            gbc.append(gb_j)
            gc.append(gc_j)
            gst.append(gst_j)
            eg.append(jnp.exp(gc_j))
            ekd_t.append(jnp.exp(glast - gst_j))
            ecd_s[g * group + j] = jnp.exp(gc_j[CHUNK - 1:CHUNK, :])

        def rows_of(j):
            return pl.ds(pl.multiple_of((g * group + j) * CHUNK, CHUNK), CHUNK)

        def cols_of(p, base):
            return slice(base + 2 * p * DK, base + (2 * p + 2) * DK)

        q2 = {pr: qkvs[rows_of(pr[0]), cols_of(pr[1], 0)] for pr in probs}
        k2 = {pr: qkvs[rows_of(pr[0]), cols_of(pr[1], HK)] for pr in probs}
        v2 = {pr: qkvs[rows_of(pr[0]), cols_of(pr[1], 2 * HK)] for pr in probs}
        k2t = {pr: jnp.concatenate([k2[pr][:, :DK], k2[pr][:, DK:]], axis=0).T for pr in probs}
        b2 = {pr: lane_pair(gbc[pr[0]][:, HEADS + 2 * pr[1]:HEADS + 2 * pr[1] + 1],
                            gbc[pr[0]][:, HEADS + 2 * pr[1] + 1:HEADS + 2 * pr[1] + 2]) for pr in probs}
        eg2 = {pr: lane_pair(eg[pr[0]][:, 2 * pr[1]:2 * pr[1] + 1],
                             eg[pr[0]][:, 2 * pr[1] + 1:2 * pr[1] + 2]) for pr in probs}
        kb2 = {pr: k2[pr] * b2[pr] for pr in probs}
        kq = {pr: _bdot(jnp.concatenate([kb2[pr], q2[pr]], axis=0), block_diag(k2t[pr].astype(BF16)))
              for pr in probs}
        decay = {}
        for pr in probs:
            j, p = pr
            gcol = jnp.where(lo, gc[j][:, 2 * p:2 * p + 1], gc[j][:, 2 * p + 1:2 * p + 2])
            decay[pr] = jnp.exp(jnp.where(incl, gcol - gst[j][2 * p:2 * p + 1, :], -jnp.inf))
        a_kk = {pr: jnp.where(strict, kq[pr][:CHUNK] * decay[pr], 0.0) for pr in probs}
        for pr in probs:
            j, p = pr
            aqk_s[g * group + j, p] = jnp.where(incl, kq[pr][CHUNK:] * decay[pr], 0.0).astype(BF16)
            kdt_s[g * group + j, p] = (k2t[pr] * ekd_t[j][2 * p:2 * p + 1, :]).astype(BF16)
        t = {pr: eye - a_kk[pr] * masks_ref[0] for pr in probs}
        for lvl in range(1, N_LEVELS):
            tb = {pr: t[pr].astype(BF16) for pr in probs}
            x = {pr: jnp.dot((a_kk[pr] * masks_ref[lvl]).astype(BF16), block_diag(tb[pr]),
                             preferred_element_type=F32) for pr in probs}
            t = {pr: t[pr] - jnp.dot(tb[pr], block_diag(x[pr].astype(BF16)),
                                     preferred_element_type=F32) for pr in probs}
        for pr in probs:
            j, p = pr
            rv = (v2[pr] * b2[pr]).astype(BF16)
            rk = (kb2[pr] * eg2[pr]).astype(BF16)
            rhs = block_diag_wide(jnp.concatenate([rv[:, :DV], rk[:, :DK]], axis=1),
                                  jnp.concatenate([rv[:, DV:], rk[:, DK:]], axis=1))
            sol = jnp.dot(t[pr].astype(BF16), rhs, preferred_element_type=F32)
            u_s[g * group + j, p] = jnp.concatenate([sol[:, :DV], sol[:, DV + DK:2 * DV + DK]], axis=1)
            w2 = jnp.concatenate([sol[:, DV:DV + DK], sol[:, 2 * DV + DK:]], axis=1)
            wq_s[g * group + j, p] = jnp.concatenate([w2, q2[pr] * eg2[pr]], axis=0).astype(BF16)
        return carry

    lax.fori_loop(0, nc // group, prep_body, 0)

    def scan_body(c, carry):
        ps = range(HEADS // 2)
        rs = pl.ds(c * CHUNK, CHUNK)
        ecd = ecd_s[c]
        s = [state[p] for p in ps]
        sb = [s[p].astype(BF16) for p in ps]
        ws_qs = [jnp.dot(wq_s[c, p], block_diag_wide(sb[p][:, :DV], sb[p][:, DV:]),
                         preferred_element_type=F32) for p in ps]
        vb = [(u_s[c, p] - ws_qs[p][:CHUNK]).astype(BF16) for p in ps]
        bdv = [block_diag_wide(vb[p][:, :DV], vb[p][:, DV:]) for p in ps]
        o = [ws_qs[p][CHUNK:] + jnp.dot(aqk_s[c, p], bdv[p], preferred_element_type=F32) for p in ps]
        for p in ps:
            cd = lane_pair(ecd[:, 2 * p:2 * p + 1], ecd[:, 2 * p + 1:2 * p + 2])
            state[p] = s[p] * cd + jnp.dot(kdt_s[c, p], bdv[p], preferred_element_type=F32)
        for p in ps:
            for half in range(2):
                h = 2 * p + half
                on = _rmsnorm(o[p][:, half * DV:(half + 1) * DV], nw_ref[...])
                o_ref[rs, h * DV:(h + 1) * DV] = (on * zbuf[rs, h * DV:(h + 1) * DV]).astype(o_ref.dtype)
        return carry

    for c in range(nc):
        scan_body(c, 0)

    @pl.when(i == pl.num_programs(1) - 1)
    def _():
        sout_ref[...] = state[...]


def _ffn_kernel(h_ref, c2_ref, o2_ref, hg_in_ref, hv_in_ref, nmw_ref, wgate_ref, bgate_ref,
                wco_ref, wdo_ref, wout_ref, nfw_ref, wupg_ref, wupv_ref, dwg_ref, dwv_ref,
                dbg_ref, dbv_ref, wdown_ref, nlw_ref, out_ref, hg_out_ref, hv_out_ref,
                gbuf0, vbuf0, gbuf1, vbuf1, halo_g, halo_v, *, tl, head_pass):
    i = pl.program_id(1)

    @pl.when(i == 0)
    def _():
        halo_g[...] = hg_in_ref[...]
        halo_v[...] = hv_in_ref[...]

    h = h_ref[...]
    gates = _sigmoid(jnp.dot(_rmsnorm(h, nmw_ref[...]).astype(BF16), wgate_ref[...],
                             preferred_element_type=F32) + bgate_ref[...])
    yc = jnp.dot(c2_ref[...], wco_ref[...], preferred_element_type=F32)
    yd = jnp.dot(o2_ref[...], wdo_ref[...], preferred_element_type=F32)
    mix = gates[:, :D_MODEL] * yc + gates[:, D_MODEL:] * yd
    h1 = h + _bdot(mix, wout_ref[...])
    if head_pass:
        h1 = jnp.where(_pad_row_mask(h1.shape), h1, 0.0)
    u2 = _rmsnorm(h1, nfw_ref[...]).astype(BF16)

    first = FFN_HALO - (FFN_CONV_K - 1)
    bounds = [sum(FF_BLOCKS[:k]) for k in range(len(FF_BLOCKS) + 1)]
    bufs = ((gbuf0, vbuf0), (gbuf1, vbuf1))

    def up_project(k):
        cs = slice(bounds[k], bounds[k + 1])
        width = FF_BLOCKS[k]
        for buf, halo, wup in ((bufs[k % 2][0], halo_g, wupg_ref), (bufs[k % 2][1], halo_v, wupv_ref)):
            buf[0:FFN_HALO, 0:width] = halo[:, cs]
            buf[FFN_HALO:FFN_HALO + tl, 0:width] = jnp.dot(u2, wup[:, cs], preferred_element_type=F32)
            halo[:, cs] = buf[tl:tl + FFN_HALO, 0:width]

    def conv_glu(k):
        cs = slice(bounds[k], bounds[k + 1])
        width = FF_BLOCKS[k]
        ys = []
        for buf, dw, db in ((bufs[k % 2][0], dwg_ref, dbg_ref), (bufs[k % 2][1], dwv_ref, dbv_ref)):
            y = db[:, cs] + dw[0:1, cs] * buf[first:first + tl, 0:width]
            for j in range(1, FFN_CONV_K):
                y = y + dw[j:j + 1, cs] * buf[first + j:first + j + tl, 0:width]
            ys.append(y)
        return (_silu(ys[0]) * ys[1]).astype(BF16)

    acc = h1
    up_project(0)
    for k in range(len(FF_BLOCKS)):
        if k + 1 < len(FF_BLOCKS):
            up_project(k + 1)
        acc = acc + jnp.dot(conv_glu(k), wdown_ref[bounds[k]:bounds[k + 1], :],
                            preferred_element_type=F32)
    out_ref[...] = _rmsnorm(acc, nlw_ref[...])

    @pl.when(i == pl.num_programs(1) - 1)
    def _():
        hg_out_ref[...] = halo_g[...]
        hv_out_ref[...] = halo_v[...]


def _const_spec(shape):
    nd = len(shape)
    return pl.BlockSpec(shape, lambda b, i: (0,) * nd, pipeline_mode=pl.Buffered(1))


def _params(sem):
    return pltpu.CompilerParams(dimension_semantics=sem, vmem_limit_bytes=VMEM_LIMIT)


def _level_masks():
    r = jnp.arange(CHUNK)[:, None]
    c = jnp.arange(CHUNK)[None, :]
    masks = []
    for lvl in range(N_LEVELS):
        s = 1 << lvl
        masks.append(((r // (2 * s)) == (c // (2 * s))) & ((r // s) % 2 == 1) & ((c // s) % 2 == 0))
    return jnp.tile(jnp.stack(masks).astype(F32), (1, 1, LANE // CHUNK))


def _layer(h, carries, wts, *, tl, head_pass):
    bsz, length, _ = h.shape
    assert length % tl == 0
    grid = (bsz, length // tl)
    conf_halo, qkv_halo, s0, ffn_halo_g, ffn_halo_v = carries
    tile = lambda width: pl.BlockSpec((None, tl, width), lambda b, i: (b, i, 0))
    cs = lambda a: _const_spec(a.shape)
    sds = jax.ShapeDtypeStruct
    tag = "head" if head_pass else "body"

    c, qkv, gb, qkv_tail = pl.pallas_call(
        functools.partial(_inproj_kernel, tl=tl, head_pass=head_pass),
        grid=grid,
        in_specs=[tile(D_MODEL), cs(qkv_halo), cs(wts["nmw"]), cs(wts["w_c"]), cs(wts["w_qkv"]),
                  cs(wts["w_gb"]), cs(wts["alog"]), cs(wts["dtb"]), cs(wts["dn_cw"])],
        out_specs=[tile(DC), tile(QKV), tile(LANE),
                   pl.BlockSpec((None, QKV_HALO, QKV), lambda b, i: (b, 0, 0))],
        out_shape=[sds((bsz, length, DC), F32), sds((bsz, length, QKV), F32),
                   sds((bsz, length, LANE), F32), sds((bsz, QKV_HALO, QKV), F32)],
        scratch_shapes=[pltpu.VMEM((tl + QKV_HALO, QKV_BLOCK), F32)] * (QKV // QKV_BLOCK),
        compiler_params=_params(("arbitrary", "arbitrary")),
        name="inproj_" + tag,
    )(h, qkv_halo, wts["nmw"], wts["w_c"], wts["w_qkv"], wts["w_gb"], wts["alog"], wts["dtb"],
      wts["dn_cw"])

    c2 = pl.pallas_call(
        functools.partial(_conf_kernel, tl=tl),
        grid=grid,
        in_specs=[tile(DC), cs(conf_halo), cs(wts["conf_w"]), cs(wts["conf_b"]),
                  cs(wts["ln_w"]), cs(wts["ln_b"])],
        out_specs=tile(DC),
        out_shape=sds((bsz, length, DC), BF16),
        scratch_shapes=[pltpu.VMEM((tl + CONF_HALO, DC), F32), pltpu.VMEM((tl, DC), F32)],
        compiler_params=_params(("arbitrary", "arbitrary")),
        name="conf_conv_" + tag,
    )(c, conf_halo, wts["conf_w"], wts["conf_b"], wts["ln_w"], wts["ln_b"])

    nc = tl // CHUNK
    group = min(CHUNK_GROUP, nc)
    o2, s_out = pl.pallas_call(
        functools.partial(_gdn_kernel, tl=tl, group=group),
        grid=grid,
        in_specs=[tile(QKV), tile(D_MODEL), tile(LANE), cs(s0),
                  cs(wts["dn_nw"]), cs(wts["nmw"]), cs(wts["w_z"]), cs(wts["tri"]), cs(wts["masks"])],
        out_specs=[tile(HV), pl.BlockSpec((None,) + STATE_SHAPE, lambda b, i: (b, 0, 0, 0))],
        out_shape=[sds((bsz, length, HV), BF16), sds((bsz,) + STATE_SHAPE, F32)],
        scratch_shapes=[pltpu.VMEM((tl, HV), F32),
                        pltpu.VMEM((nc, HEADS // 2, 2 * CHUNK, 2 * DK), BF16),
                        pltpu.VMEM((nc, HEADS // 2, CHUNK, 2 * DV), F32),
                        pltpu.VMEM((nc, HEADS // 2, CHUNK, LANE), BF16),
                        pltpu.VMEM((nc, HEADS // 2, DK, LANE), BF16),
                        pltpu.VMEM((nc, 1, LANE), F32),
                        pltpu.VMEM(STATE_SHAPE, F32)],
        compiler_params=_params(("arbitrary", "arbitrary")),
        name="gdn_" + tag,
    )(qkv, h, gb, s0, wts["dn_nw"], wts["nmw"], wts["w_z"], wts["tri"], wts["masks"])

    halo_spec = pl.BlockSpec((None, FFN_HALO, D_FF), lambda b, i: (b, 0, 0))
    out, hg, hv = pl.pallas_call(
        functools.partial(_ffn_kernel, tl=tl, head_pass=head_pass),
        grid=grid,
        in_specs=[tile(D_MODEL), tile(DC), tile(HV), cs(ffn_halo_g), cs(ffn_halo_v),
                  cs(wts["nmw"]), cs(wts["w_gate"]), cs(wts["b_gate"]), cs(wts["w_co"]),
                  cs(wts["w_do"]), cs(wts["w_out"]), cs(wts["nfw"]), cs(wts["w_upg"]),
                  cs(wts["w_upv"]), cs(wts["dw_g"]), cs(wts["dw_v"]), cs(wts["db_g"]),
                  cs(wts["db_v"]), cs(wts["w_down"]), cs(wts["nlw"])],
        out_specs=[tile(D_MODEL), halo_spec, halo_spec],
        out_shape=[sds((bsz, length, D_MODEL), F32), sds((bsz, FFN_HALO, D_FF), F32),
                   sds((bsz, FFN_HALO, D_FF), F32)],
        scratch_shapes=[pltpu.VMEM((tl + FFN_HALO, max(FF_BLOCKS)), F32)] * 4
        + [pltpu.VMEM((FFN_HALO, D_FF), F32)] * 2,
        compiler_params=_params(("arbitrary", "arbitrary")),
        name="merge_ffn_" + tag,
    )(h, c2, o2, ffn_halo_g, ffn_halo_v, wts["nmw"], wts["w_gate"], wts["b_gate"], wts["w_co"],
      wts["w_do"], wts["w_out"], wts["nfw"], wts["w_upg"], wts["w_upv"], wts["dw_g"], wts["dw_v"],
      wts["db_g"], wts["db_v"], wts["w_down"], wts["nlw"])

    new_carries = (c[0, length - CONF_HALO:], qkv_tail[0], s_out[0], hg[0], hv[0])
    return out, new_carries


def kernel(x, meta_tokens, norm_mix_w, w_in, b_gate, conf_dw_w, conf_dw_b, conf_ln_w, conf_ln_b, w_conf_out, dn_conv_w, dn_A_log, dn_dt_bias, dn_norm_w, w_dn_out, w_out, norm_ffn_w, w_up, ffn_dw_w, ffn_dw_b, w_down, norm_final_w):
    bsz, seq, _ = x.shape
    assert seq % BODY_ROWS == 0, "sequence must be a whole number of body tiles"
    assert w_in.shape[0] == 1, "single-layer block"

    wi = w_in[0]
    o_q = 2 * DC
    o_z = o_q + QKV
    o_a = o_z + HV
    o_gate = o_a + 2 * HEADS
    row = lambda v: v.reshape(1, -1).astype(F32)
    lane_pad = lambda v: jnp.pad(row(v), ((0, 0), (0, LANE - v.shape[-1])))
    wu = w_up[0]
    wts = {
        "nmw": row(norm_mix_w[0]),
        "w_c": wi[:, :o_q].astype(BF16),
        "w_qkv": wi[:, o_q:o_z].astype(BF16),
        "w_z": wi[:, o_z:o_a].astype(BF16),
        "w_gb": jnp.pad(wi[:, o_a:o_gate], ((0, 0), (0, LANE - 2 * HEADS))).astype(BF16),
        "w_gate": wi[:, o_gate:].astype(BF16),
        "b_gate": row(b_gate[0]),
        "alog": lane_pad(dn_A_log[0]),
        "dtb": lane_pad(dn_dt_bias[0]),
        "conf_w": conf_dw_w[0].astype(F32),
        "conf_b": row(conf_dw_b[0]),
        "ln_w": row(conf_ln_w[0]),
        "ln_b": row(conf_ln_b[0]),
        "dn_cw": dn_conv_w[0].astype(F32),
        "dn_nw": row(dn_norm_w[0]),
        "tri": (jnp.arange(CHUNK)[:, None] >= jnp.arange(CHUNK)[None, :]).astype(BF16),
        "masks": _level_masks(),
        "w_co": w_conf_out[0].astype(BF16),
        "w_do": w_dn_out[0].astype(BF16),
        "w_out": w_out[0].astype(BF16),
        "nfw": row(norm_ffn_w[0]),
        "w_upg": wu[:, :D_FF].astype(BF16),
        "w_upv": wu[:, D_FF:].astype(BF16),
        "dw_g": ffn_dw_w[0][:, :D_FF].astype(F32),
        "dw_v": ffn_dw_w[0][:, D_FF:].astype(F32),
        "db_g": row(ffn_dw_b[0][:D_FF]),
        "db_v": row(ffn_dw_b[0][D_FF:]),
        "w_down": w_down[0].astype(BF16),
        "nlw": row(norm_final_w),
    }

    zero_carries = (jnp.zeros((CONF_HALO, DC), F32), jnp.zeros((QKV_HALO, QKV), F32),
                    jnp.zeros(STATE_SHAPE, F32), jnp.zeros((FFN_HALO, D_FF), F32),
                    jnp.zeros((FFN_HALO, D_FF), F32))
    head = jnp.concatenate([jnp.zeros((PAD, D_MODEL), x.dtype), meta_tokens.astype(x.dtype)])[None]
    _, carries = _layer(head, zero_carries, wts, tl=CHUNK, head_pass=True)
    out, _ = _layer(x, carries, wts, tl=BODY_ROWS, head_pass=False)
    return out
```

```python
import functools

import jax
import jax.numpy as jnp
from jax import lax
from jax.experimental import pallas as pl
from jax.experimental.pallas import tpu as pltpu

D_MODEL = 1024
N_META = 16
CHUNK = 64
PAD = CHUNK - N_META
DC = D_MODEL
CONF_K = 31
HEADS = 8
DK = 128
DV = 128
HK = HEADS * DK
HV = HEADS * DV
QKV = 2 * HK + HV
DN_CONV_K = 4
D_FF = 2816
FFN_CONV_K = 3
EPS = 1e-6

LANE = 128
SUBLANE = 8
BODY_ROWS = 512
MXU_DIM = 256
FF_BLOCKS = (3 * MXU_DIM, 3 * MXU_DIM, 3 * MXU_DIM, 2 * MXU_DIM)
QKV_BLOCK = 2 * MXU_DIM
ROW_STRIDE = 2
CHUNK_GROUP = 8
VMEM_LIMIT = 56 * 1024 * 1024

CONF_HALO = 32
QKV_HALO = SUBLANE
FFN_HALO = SUBLANE
N_LEVELS = 6
STATE_SHAPE = (HEADS // 2, DK, 2 * DV)

F32 = jnp.float32
BF16 = jnp.bfloat16

assert sum(FF_BLOCKS) == D_FF


def _bdot(a, b):
    return jnp.dot(a.astype(BF16), b.astype(BF16), preferred_element_type=F32)


NEG_LOG2_E = -1.4426950408889634


def _sigmoid(x):
    return 1.0 / (1.0 + jnp.exp2(x * NEG_LOG2_E))


def _silu(x):
    return x * _sigmoid(x)


def _softplus(x):
    return jnp.maximum(x, 0.0) + jnp.log(1.0 + jnp.exp(-jnp.abs(x)))


def _rmsnorm(x, w):
    return x * lax.rsqrt(jnp.mean(x * x, axis=-1, keepdims=True) + EPS) * w


def _split3(x):
    hi = x.astype(BF16)
    r = x - hi.astype(F32)
    mid = r.astype(BF16)
    lo = (r - mid.astype(F32)).astype(BF16)
    return hi, mid, lo


def _pad_row_mask(shape):
    return lax.broadcasted_iota(jnp.int32, shape, 0) >= PAD


def _inproj_kernel(h_ref, halo_ref, nw_ref, wc_ref, wqkv_ref, wgb_ref, alog_ref, dtb_ref, cw_ref,
                   c_ref, qkv_ref, gb_ref, tail_ref, stage, *, tl, head_pass):
    i = pl.program_id(1)
    nblk = QKV // QKV_BLOCK
    per_blk = QKV_BLOCK // LANE
    u = _rmsnorm(h_ref[...], nw_ref[...]).astype(BF16)

    @pl.when(i == 0)
    def _():
        for lb in range(QKV // LANE):
            stage[lb, 0:QKV_HALO, :] = halo_ref[:, lb * LANE:(lb + 1) * LANE]

    @pl.when(i > 0)
    def _():
        for lb in range(QKV // LANE):
            stage[lb, 0:QKV_HALO, :] = stage[lb, tl:tl + QKV_HALO, :]

    def project(cb):
        res = jnp.dot(u, wqkv_ref[:, cb * QKV_BLOCK:(cb + 1) * QKV_BLOCK], preferred_element_type=F32)
        for lbb in range(per_blk):
            stage[cb * per_blk + lbb, QKV_HALO:QKV_HALO + tl, :] = res[:, lbb * LANE:(lbb + 1) * LANE]

    def conv_act(cb):
        first = QKV_HALO - (DN_CONV_K - 1)
        for lb in range(cb * per_blk, (cb + 1) * per_blk):
            ls = slice(lb * LANE, (lb + 1) * LANE)
            for r0 in range(0, tl, SUBLANE * ROW_STRIDE):
                for s in range(ROW_STRIDE):
                    taps = [stage[lb, pl.ds(r0 + s + first + j, SUBLANE, stride=ROW_STRIDE), :]
                            for j in range(DN_CONV_K)]
                    acc = cw_ref[0:1, ls] * taps[0]
                    for j in range(1, DN_CONV_K):
                        acc = acc + cw_ref[j:j + 1, ls] * taps[j]
                    y = _silu(acc)
                    if lb < 2 * HEADS:
                        inv = lax.rsqrt(jnp.sum(y * y, axis=-1, keepdims=True) + EPS)
                        y = y * (inv * (DK ** -0.5) if lb < HEADS else inv)
                    qkv_ref[lb, pl.ds(r0 + s, SUBLANE, stride=ROW_STRIDE), :] = y

    project(0)
    for cb in range(nblk):
        if cb + 1 < nblk:
            project(cb + 1)
        else:
            cin = jnp.dot(u, wc_ref[...], preferred_element_type=F32)
            c_ref[...] = cin[:, :DC] * _sigmoid(cin[:, DC:])
        conv_act(cb)

    p = jnp.dot(u, wgb_ref[...], preferred_element_type=F32)
    g = -jnp.exp(alog_ref[...]) * _softplus(p + dtb_ref[...])
    lane = lax.broadcasted_iota(jnp.int32, p.shape, 1)
    val = jnp.where(lane < HEADS, g, jnp.where(lane < 2 * HEADS, _sigmoid(p), 0.0))
    if head_pass:
        val = jnp.where(_pad_row_mask(p.shape), val, 0.0)
    gb_ref[...] = val

    @pl.when(i == pl.num_programs(1) - 1)
    def _():
        for lb in range(QKV // LANE):
            tail_ref[:, lb * LANE:(lb + 1) * LANE] = stage[lb, tl:tl + QKV_HALO, :]


def _conf_kernel(c_ref, halo_ref, w_ref, b_ref, lnw_ref, lnb_ref, o_ref, cbuf, ybuf, *, tl):
    i = pl.program_id(1)
    nlb = DC // LANE

    @pl.when(i == 0)
    def _():
        for lb in range(nlb):
            cbuf[lb, 0:CONF_HALO, :] = halo_ref[:, lb * LANE:(lb + 1) * LANE]

    @pl.when(i > 0)
    def _():
        for lb in range(nlb):
            cbuf[lb, 0:CONF_HALO, :] = cbuf[lb, tl:tl + CONF_HALO, :]

    first = CONF_HALO - (CONF_K - 1)
    span = SUBLANE * ROW_STRIDE
    for lb in range(nlb):
        ls = slice(lb * LANE, (lb + 1) * LANE)
        cbuf[lb, CONF_HALO:CONF_HALO + tl, :] = c_ref[:, ls]
        for base in range(0, tl, span):
            for s in range(ROW_STRIDE):
                y = jnp.broadcast_to(b_ref[:, ls], (SUBLANE, LANE))
                for j in range(CONF_K):
                    rows = pl.ds(base + s + first + j, SUBLANE, stride=ROW_STRIDE)
                    y = y + w_ref[j:j + 1, ls] * cbuf[lb, rows, :]
                ybuf[lb, pl.ds(base + s, SUBLANE, stride=ROW_STRIDE), :] = y
    y = jnp.concatenate([ybuf[lb] for lb in range(nlb)], axis=1)
    mu = jnp.mean(y, axis=-1, keepdims=True)
    yc = y - mu
    yn = yc * lax.rsqrt(jnp.mean(yc * yc, axis=-1, keepdims=True) + EPS)
    o_ref[...] = _silu(yn * lnw_ref[...] + lnb_ref[...]).astype(o_ref.dtype)


def _gdn_kernel(qkvs, h_ref, gb_ref, s0_ref, nw_ref, nmw_ref, wz_ref, tri_ref, masks_ref,
                o_ref, sout_ref, zbuf, wq_s, u_s, aqk_s, kdt_s, ecd_s, state, *, tl, group):
    i = pl.program_id(1)
    nc = tl // CHUNK

    @pl.when(i == 0)
    def _():
        state[...] = s0_ref[...]

    zbuf[...] = _silu(jnp.dot(_rmsnorm(h_ref[...], nmw_ref[...]).astype(BF16), wz_ref[...],
                              preferred_element_type=F32))

    lane = lax.broadcasted_iota(jnp.int32, (CHUNK, LANE), 1)
    ri = lax.broadcasted_iota(jnp.int32, (CHUNK, LANE), 0)
    ci = jnp.bitwise_and(lane, CHUNK - 1)
    lo = lane < CHUNK
    lo2 = lax.broadcasted_iota(jnp.int32, (2 * CHUNK, LANE), 1) < CHUNK

    def block_diag(m):
        keep = lo if m.shape[0] == CHUNK else lo2
        zero = jnp.zeros_like(m)
        return jnp.concatenate([jnp.where(keep, m, zero), jnp.where(keep, zero, m)], axis=0)

    def block_diag_wide(a, b):
        za = jnp.zeros_like(a)
        return jnp.concatenate([jnp.concatenate([a, za], axis=1), jnp.concatenate([za, b], axis=1)],
                               axis=0)

    def lane_pair(col0, col1):
        r = col0.shape[0]
        return jnp.concatenate([jnp.broadcast_to(col0, (r, LANE)), jnp.broadcast_to(col1, (r, LANE))],
                               axis=1)

    def prep_body(g, carry):
        probs = [(j, p) for j in range(group) for p in range(HEADS // 2)]
        incl = ri >= ci
        strict = ri > ci
        eye = jnp.where(ri == ci, 1.0, 0.0)
        tri = tri_ref[...]
        gbc, gc, gst, eg, ekd_t = [], [], [], [], []
        for j in range(group):
            rs = pl.ds(pl.multiple_of((g * group + j) * CHUNK, CHUNK), CHUNK)
            gb_j = gb_ref[rs, :]
            gc_j = sum(jnp.dot(tri, p, preferred_element_type=F32) for p in _split3(gb_j))
            gst_j = jnp.concatenate([gc_j, pltpu.roll(gc_j, LANE - 1, axis=1)], axis=0).T
            glast = jnp.where(lo2, gst_j[:, CHUNK - 1:CHUNK], gst_j[:, LANE - 1:LANE])
            gbc.append(gb_j)
            gc.append(gc_j)
            gst.append(gst_j)
            eg.append(jnp.exp(gc_j))
            ekd_t.append(jnp.exp(glast - gst_j))
            ecd_s[g * group + j] = jnp.exp(gc_j[CHUNK - 1:CHUNK, :])

        def rows_of(j):
            return pl.ds(pl.multiple_of((g * group + j) * CHUNK, CHUNK), CHUNK)

        def head_pair(pr, first_head):
            rs, h0 = rows_of(pr[0]), first_head + 2 * pr[1]
            return jnp.concatenate([qkvs[h0, rs, :], qkvs[h0 + 1, rs, :]], axis=1)

        q2 = {pr: head_pair(pr, 0) for pr in probs}
        k2 = {pr: head_pair(pr, HEADS) for pr in probs}
        v2 = {pr: head_pair(pr, 2 * HEADS) for pr in probs}
        k2t = {pr: jnp.concatenate([k2[pr][:, :DK], k2[pr][:, DK:]], axis=0).T for pr in probs}
        b2 = {pr: lane_pair(gbc[pr[0]][:, HEADS + 2 * pr[1]:HEADS + 2 * pr[1] + 1],
                            gbc[pr[0]][:, HEADS + 2 * pr[1] + 1:HEADS + 2 * pr[1] + 2]) for pr in probs}
        eg2 = {pr: lane_pair(eg[pr[0]][:, 2 * pr[1]:2 * pr[1] + 1],
                             eg[pr[0]][:, 2 * pr[1] + 1:2 * pr[1] + 2]) for pr in probs}
        kb2 = {pr: k2[pr] * b2[pr] for pr in probs}
        kq = {pr: _bdot(jnp.concatenate([kb2[pr], q2[pr]], axis=0), block_diag(k2t[pr].astype(BF16)))
              for pr in probs}
        decay = {}
        for pr in probs:
            j, p = pr
            gcol = jnp.where(lo, gc[j][:, 2 * p:2 * p + 1], gc[j][:, 2 * p + 1:2 * p + 2])
            decay[pr] = jnp.exp(jnp.where(incl, gcol - gst[j][2 * p:2 * p + 1, :], -jnp.inf))
        a_kk = {pr: jnp.where(strict, kq[pr][:CHUNK] * decay[pr], 0.0) for pr in probs}
        for pr in probs:
            j, p = pr
            aqk_s[g * group + j, p] = jnp.where(incl, kq[pr][CHUNK:] * decay[pr], 0.0).astype(BF16)
            kdt_s[g * group + j, p] = (k2t[pr] * ekd_t[j][2 * p:2 * p + 1, :]).astype(BF16)
        t = {pr: eye - a_kk[pr] * masks_ref[0] for pr in probs}
        for lvl in range(1, N_LEVELS):
            tb = {pr: t[pr].astype(BF16) for pr in probs}
            x = {pr: jnp.dot((a_kk[pr] * masks_ref[lvl]).astype(BF16), block_diag(tb[pr]),
                             preferred_element_type=F32) for pr in probs}
            t = {pr: t[pr] - jnp.dot(tb[pr], block_diag(x[pr].astype(BF16)),
                                     preferred_element_type=F32) for pr in probs}
        for pr in probs:
            j, p = pr
            rv = (v2[pr] * b2[pr]).astype(BF16)
            rk = (kb2[pr] * eg2[pr]).astype(BF16)
            rhs = block_diag_wide(jnp.concatenate([rv[:, :DV], rk[:, :DK]], axis=1),
                                  jnp.concatenate([rv[:, DV:], rk[:, DK:]], axis=1))
            sol = jnp.dot(t[pr].astype(BF16), rhs, preferred_element_type=F32)
            u_s[g * group + j, p] = jnp.concatenate([sol[:, :DV], sol[:, DV + DK:2 * DV + DK]], axis=1)
            w2 = jnp.concatenate([sol[:, DV:DV + DK], sol[:, 2 * DV + DK:]], axis=1)
            wq_s[g * group + j, p] = jnp.concatenate([w2, q2[pr] * eg2[pr]], axis=0).astype(BF16)
        return carry

    lax.fori_loop(0, nc // group, prep_body, 0)

    def scan_body(c, carry):
        ps = range(HEADS // 2)
        rs = pl.ds(c * CHUNK, CHUNK)
        ecd = ecd_s[c]
        s = [state[p] for p in ps]
        sb = [s[p].astype(BF16) for p in ps]
        ws_qs = [jnp.dot(wq_s[c, p], block_diag_wide(sb[p][:, :DV], sb[p][:, DV:]),
                         preferred_element_type=F32) for p in ps]
        vb = [(u_s[c, p] - ws_qs[p][:CHUNK]).astype(BF16) for p in ps]
        bdv = [block_diag_wide(vb[p][:, :DV], vb[p][:, DV:]) for p in ps]
        o = [ws_qs[p][CHUNK:] + jnp.dot(aqk_s[c, p], bdv[p], preferred_element_type=F32) for p in ps]
        for p in ps:
            cd = lane_pair(ecd[:, 2 * p:2 * p + 1], ecd[:, 2 * p + 1:2 * p + 2])
            state[p] = s[p] * cd + jnp.dot(kdt_s[c, p], bdv[p], preferred_element_type=F32)
        for p in ps:
            for half in range(2):
                h = 2 * p + half
                on = _rmsnorm(o[p][:, half * DV:(half + 1) * DV], nw_ref[...])
                o_ref[rs, h * DV:(h + 1) * DV] = (on * zbuf[rs, h * DV:(h + 1) * DV]).astype(o_ref.dtype)
        return carry

    for c in range(nc):
        scan_body(c, 0)

    @pl.when(i == pl.num_programs(1) - 1)
    def _():
        sout_ref[...] = state[...]


def _ffn_kernel(h_ref, c2_ref, o2_ref, hg_in_ref, hv_in_ref, nmw_ref, wgate_ref, bgate_ref,
                wco_ref, wdo_ref, wout_ref, nfw_ref, wupg_ref, wupv_ref, dwg_ref, dwv_ref,
                dbg_ref, dbv_ref, wdown_ref, nlw_ref, out_ref, hg_out_ref, hv_out_ref,
                gbuf0, vbuf0, gbuf1, vbuf1, halo_g, halo_v, *, tl, head_pass):
    i = pl.program_id(1)

    @pl.when(i == 0)
    def _():
        halo_g[...] = hg_in_ref[...]
        halo_v[...] = hv_in_ref[...]

    h = h_ref[...]
    gates = _sigmoid(jnp.dot(_rmsnorm(h, nmw_ref[...]).astype(BF16), wgate_ref[...],
                             preferred_element_type=F32) + bgate_ref[...])
    yc = jnp.dot(c2_ref[...], wco_ref[...], preferred_element_type=F32)
    yd = jnp.dot(o2_ref[...], wdo_ref[...], preferred_element_type=F32)
    mix = gates[:, :D_MODEL] * yc + gates[:, D_MODEL:] * yd
    h1 = h + _bdot(mix, wout_ref[...])
    if head_pass:
        h1 = jnp.where(_pad_row_mask(h1.shape), h1, 0.0)
    u2 = _rmsnorm(h1, nfw_ref[...]).astype(BF16)

    first = FFN_HALO - (FFN_CONV_K - 1)
    bounds = [sum(FF_BLOCKS[:k]) for k in range(len(FF_BLOCKS) + 1)]
    bufs = ((gbuf0, vbuf0), (gbuf1, vbuf1))

    def up_project(k):
        cs = slice(bounds[k], bounds[k + 1])
        width = FF_BLOCKS[k]
        for buf, halo, wup in ((bufs[k % 2][0], halo_g, wupg_ref), (bufs[k % 2][1], halo_v, wupv_ref)):
            buf[0:FFN_HALO, 0:width] = halo[:, cs]
            buf[FFN_HALO:FFN_HALO + tl, 0:width] = jnp.dot(u2, wup[:, cs], preferred_element_type=F32)
            halo[:, cs] = buf[tl:tl + FFN_HALO, 0:width]

    def conv_glu(k):
        cs = slice(bounds[k], bounds[k + 1])
        width = FF_BLOCKS[k]
        ys = []
        for buf, dw, db in ((bufs[k % 2][0], dwg_ref, dbg_ref), (bufs[k % 2][1], dwv_ref, dbv_ref)):
            y = db[:, cs] + dw[0:1, cs] * buf[first:first + tl, 0:width]
            for j in range(1, FFN_CONV_K):
                y = y + dw[j:j + 1, cs] * buf[first + j:first + j + tl, 0:width]
            ys.append(y)
        return (_silu(ys[0]) * ys[1]).astype(BF16)

    acc = h1
    up_project(0)
    for k in range(len(FF_BLOCKS)):
        if k + 1 < len(FF_BLOCKS):
            up_project(k + 1)
        acc = acc + jnp.dot(conv_glu(k), wdown_ref[bounds[k]:bounds[k + 1], :],
                            preferred_element_type=F32)
    out_ref[...] = _rmsnorm(acc, nlw_ref[...])

    @pl.when(i == pl.num_programs(1) - 1)
    def _():
        hg_out_ref[...] = halo_g[...]
        hv_out_ref[...] = halo_v[...]


def _const_spec(shape):
    nd = len(shape)
    return pl.BlockSpec(shape, lambda b, i: (0,) * nd, pipeline_mode=pl.Buffered(1))


def _params(sem):
    return pltpu.CompilerParams(dimension_semantics=sem, vmem_limit_bytes=VMEM_LIMIT)


def _level_masks():
    r = jnp.arange(CHUNK)[:, None]
    c = jnp.arange(CHUNK)[None, :]
    masks = []
    for lvl in range(N_LEVELS):
        s = 1 << lvl
        masks.append(((r // (2 * s)) == (c // (2 * s))) & ((r // s) % 2 == 1) & ((c // s) % 2 == 0))
    return jnp.tile(jnp.stack(masks).astype(F32), (1, 1, LANE // CHUNK))


def _layer(h, carries, wts, *, tl, head_pass):
    bsz, length, _ = h.shape
    assert length % tl == 0
    grid = (bsz, length // tl)
    conf_halo, qkv_halo, s0, ffn_halo_g, ffn_halo_v = carries
    tile = lambda width: pl.BlockSpec((None, tl, width), lambda b, i: (b, i, 0))
    heads_tile = pl.BlockSpec((None, QKV // LANE, tl, LANE), lambda b, i: (b, 0, i, 0))
    cs = lambda a: _const_spec(a.shape)
    sds = jax.ShapeDtypeStruct
    tag = "head" if head_pass else "body"

    c, qkv, gb, qkv_tail = pl.pallas_call(
        functools.partial(_inproj_kernel, tl=tl, head_pass=head_pass),
        grid=grid,
        in_specs=[tile(D_MODEL), cs(qkv_halo), cs(wts["nmw"]), cs(wts["w_c"]), cs(wts["w_qkv"]),
                  cs(wts["w_gb"]), cs(wts["alog"]), cs(wts["dtb"]), cs(wts["dn_cw"])],
        out_specs=[tile(DC), heads_tile, tile(LANE),
                   pl.BlockSpec((None, QKV_HALO, QKV), lambda b, i: (b, 0, 0))],
        out_shape=[sds((bsz, length, DC), F32), sds((bsz, QKV // LANE, length, LANE), F32),
                   sds((bsz, length, LANE), F32), sds((bsz, QKV_HALO, QKV), F32)],
        scratch_shapes=[pltpu.VMEM((QKV // LANE, tl + QKV_HALO, LANE), F32)],
        compiler_params=_params(("arbitrary", "arbitrary")),
        name="inproj_" + tag,
    )(h, qkv_halo, wts["nmw"], wts["w_c"], wts["w_qkv"], wts["w_gb"], wts["alog"], wts["dtb"],
      wts["dn_cw"])

    c2 = pl.pallas_call(
        functools.partial(_conf_kernel, tl=tl),
        grid=grid,
        in_specs=[tile(DC), cs(conf_halo), cs(wts["conf_w"]), cs(wts["conf_b"]),
                  cs(wts["ln_w"]), cs(wts["ln_b"])],
        out_specs=tile(DC),
        out_shape=sds((bsz, length, DC), BF16),
        scratch_shapes=[pltpu.VMEM((DC // LANE, tl + CONF_HALO, LANE), F32),
                        pltpu.VMEM((DC // LANE, tl, LANE), F32)],
        compiler_params=_params(("arbitrary", "arbitrary")),
        name="conf_conv_" + tag,
    )(c, conf_halo, wts["conf_w"], wts["conf_b"], wts["ln_w"], wts["ln_b"])

    nc = tl // CHUNK
    group = min(CHUNK_GROUP, nc)
    o2, s_out = pl.pallas_call(
        functools.partial(_gdn_kernel, tl=tl, group=group),
        grid=grid,
        in_specs=[heads_tile, tile(D_MODEL), tile(LANE), cs(s0),
                  cs(wts["dn_nw"]), cs(wts["nmw"]), cs(wts["w_z"]), cs(wts["tri"]), cs(wts["masks"])],
        out_specs=[tile(HV), pl.BlockSpec((None,) + STATE_SHAPE, lambda b, i: (b, 0, 0, 0))],
        out_shape=[sds((bsz, length, HV), BF16), sds((bsz,) + STATE_SHAPE, F32)],
        scratch_shapes=[pltpu.VMEM((tl, HV), F32),
                        pltpu.VMEM((nc, HEADS // 2, 2 * CHUNK, 2 * DK), BF16),
                        pltpu.VMEM((nc, HEADS // 2, CHUNK, 2 * DV), F32),
                        pltpu.VMEM((nc, HEADS // 2, CHUNK, LANE), BF16),
                        pltpu.VMEM((nc, HEADS // 2, DK, LANE), BF16),
                        pltpu.VMEM((nc, 1, LANE), F32),
                        pltpu.VMEM(STATE_SHAPE, F32)],
        compiler_params=_params(("arbitrary", "arbitrary")),
        name="gdn_" + tag,
    )(qkv, h, gb, s0, wts["dn_nw"], wts["nmw"], wts["w_z"], wts["tri"], wts["masks"])

    halo_spec = pl.BlockSpec((None, FFN_HALO, D_FF), lambda b, i: (b, 0, 0))
    out, hg, hv = pl.pallas_call(
        functools.partial(_ffn_kernel, tl=tl, head_pass=head_pass),
        grid=grid,
        in_specs=[tile(D_MODEL), tile(DC), tile(HV), cs(ffn_halo_g), cs(ffn_halo_v),
                  cs(wts["nmw"]), cs(wts["w_gate"]), cs(wts["b_gate"]), cs(wts["w_co"]),
                  cs(wts["w_do"]), cs(wts["w_out"]), cs(wts["nfw"]), cs(wts["w_upg"]),
                  cs(wts["w_upv"]), cs(wts["dw_g"]), cs(wts["dw_v"]), cs(wts["db_g"]),
                  cs(wts["db_v"]), cs(wts["w_down"]), cs(wts["nlw"])],
        out_specs=[tile(D_MODEL), halo_spec, halo_spec],
        out_shape=[sds((bsz, length, D_MODEL), F32), sds((bsz, FFN_HALO, D_FF), F32),
                   sds((bsz, FFN_HALO, D_FF), F32)],
        scratch_shapes=[pltpu.VMEM((tl + FFN_HALO, max(FF_BLOCKS)), F32)] * 4
        + [pltpu.VMEM((FFN_HALO, D_FF), F32)] * 2,
        compiler_params=_params(("arbitrary", "arbitrary")),
        name="merge_ffn_" + tag,
    )(h, c2, o2, ffn_halo_g, ffn_halo_v, wts["nmw"], wts["w_gate"], wts["b_gate"], wts["w_co"],
      wts["w_do"], wts["w_out"], wts["nfw"], wts["w_upg"], wts["w_upv"], wts["dw_g"], wts["dw_v"],
      wts["db_g"], wts["db_v"], wts["w_down"], wts["nlw"])

    new_carries = (c[0, length - CONF_HALO:], qkv_tail[0], s_out[0], hg[0], hv[0])
    return out, new_carries


def kernel(x, meta_tokens, norm_mix_w, w_in, b_gate, conf_dw_w, conf_dw_b, conf_ln_w, conf_ln_b, w_conf_out, dn_conv_w, dn_A_log, dn_dt_bias, dn_norm_w, w_dn_out, w_out, norm_ffn_w, w_up, ffn_dw_w, ffn_dw_b, w_down, norm_final_w):
    bsz, seq, _ = x.shape
    assert seq % BODY_ROWS == 0, "sequence must be a whole number of body tiles"
    assert w_in.shape[0] == 1, "single-layer block"

    wi = w_in[0]
    o_q = 2 * DC
    o_z = o_q + QKV
    o_a = o_z + HV
    o_gate = o_a + 2 * HEADS
    row = lambda v: v.reshape(1, -1).astype(F32)
    lane_pad = lambda v: jnp.pad(row(v), ((0, 0), (0, LANE - v.shape[-1])))
    wu = w_up[0]
    wts = {
        "nmw": row(norm_mix_w[0]),
        "w_c": wi[:, :o_q].astype(BF16),
        "w_qkv": wi[:, o_q:o_z].astype(BF16),
        "w_z": wi[:, o_z:o_a].astype(BF16),
        "w_gb": jnp.pad(wi[:, o_a:o_gate], ((0, 0), (0, LANE - 2 * HEADS))).astype(BF16),
        "w_gate": wi[:, o_gate:].astype(BF16),
        "b_gate": row(b_gate[0]),
        "alog": lane_pad(dn_A_log[0]),
        "dtb": lane_pad(dn_dt_bias[0]),
        "conf_w": conf_dw_w[0].astype(F32),
        "conf_b": row(conf_dw_b[0]),
        "ln_w": row(conf_ln_w[0]),
        "ln_b": row(conf_ln_b[0]),
        "dn_cw": dn_conv_w[0].astype(F32),
        "dn_nw": row(dn_norm_w[0]),
        "tri": (jnp.arange(CHUNK)[:, None] >= jnp.arange(CHUNK)[None, :]).astype(BF16),
        "masks": _level_masks(),
        "w_co": w_conf_out[0].astype(BF16),
        "w_do": w_dn_out[0].astype(BF16),
        "w_out": w_out[0].astype(BF16),
        "nfw": row(norm_ffn_w[0]),
        "w_upg": wu[:, :D_FF].astype(BF16),
        "w_upv": wu[:, D_FF:].astype(BF16),
        "dw_g": ffn_dw_w[0][:, :D_FF].astype(F32),
        "dw_v": ffn_dw_w[0][:, D_FF:].astype(F32),
        "db_g": row(ffn_dw_b[0][:D_FF]),
        "db_v": row(ffn_dw_b[0][D_FF:]),
        "w_down": w_down[0].astype(BF16),
        "nlw": row(norm_final_w),
    }

    zero_carries = (jnp.zeros((CONF_HALO, DC), F32), jnp.zeros((QKV_HALO, QKV), F32),
                    jnp.zeros(STATE_SHAPE, F32), jnp.zeros((FFN_HALO, D_FF), F32),
                    jnp.zeros((FFN_HALO, D_FF), F32))
    head = jnp.concatenate([jnp.zeros((PAD, D_MODEL), x.dtype), meta_tokens.astype(x.dtype)])[None]
    _, carries = _layer(head, zero_carries, wts, tl=CHUNK, head_pass=True)
    out, _ = _layer(x, carries, wts, tl=BODY_ROWS, head_pass=False)
    return out
```

```python
import functools

import jax
import jax.numpy as jnp
from jax import lax
from jax.experimental import pallas as pl
from jax.experimental.pallas import tpu as pltpu

D_MODEL = 1024
N_META = 16
CHUNK = 64
PAD = CHUNK - N_META
DC = D_MODEL
CONF_K = 31
HEADS = 8
DK = 128
DV = 128
HK = HEADS * DK
HV = HEADS * DV
QKV = 2 * HK + HV
DN_CONV_K = 4
D_FF = 2816
FFN_CONV_K = 3
EPS = 1e-6

LANE = 128
SUBLANE = 8
BODY_ROWS = 512
MXU_DIM = 256
FF_BLOCKS = (3 * MXU_DIM, 3 * MXU_DIM, 3 * MXU_DIM, 2 * MXU_DIM)
QKV_BLOCK = 2 * MXU_DIM
ROW_STRIDE = 2
CHUNK_GROUP = 8
VMEM_LIMIT = 56 * 1024 * 1024

CONF_HALO = 32
QKV_HALO = SUBLANE
FFN_HALO = SUBLANE
N_LEVELS = 6
STATE_SHAPE = (HEADS // 2, DK, 2 * DV)

F32 = jnp.float32
BF16 = jnp.bfloat16

assert sum(FF_BLOCKS) == D_FF


def _bdot(a, b):
    return jnp.dot(a.astype(BF16), b.astype(BF16), preferred_element_type=F32)


NEG_LOG2_E = -1.4426950408889634


def _sigmoid(x):
    return 1.0 / (1.0 + jnp.exp2(x * NEG_LOG2_E))


def _silu(x):
    return x * _sigmoid(x)


def _softplus(x):
    return jnp.maximum(x, 0.0) + jnp.log(1.0 + jnp.exp(-jnp.abs(x)))


def _rmsnorm(x, w):
    return x * lax.rsqrt(jnp.mean(x * x, axis=-1, keepdims=True) + EPS) * w


def _split3(x):
    hi = x.astype(BF16)
    r = x - hi.astype(F32)
    mid = r.astype(BF16)
    lo = (r - mid.astype(F32)).astype(BF16)
    return hi, mid, lo


def _pad_row_mask(shape):
    return lax.broadcasted_iota(jnp.int32, shape, 0) >= PAD


def _inproj_kernel(h_ref, halo_ref, nw_ref, wc_ref, wqkv_ref, wgb_ref, alog_ref, dtb_ref, cw_ref,
                   c_ref, qkv_ref, gb_ref, tail_ref, stage, *, tl, head_pass):
    i = pl.program_id(1)
    nblk = QKV // QKV_BLOCK
    per_blk = QKV_BLOCK // LANE
    u = _rmsnorm(h_ref[...], nw_ref[...]).astype(BF16)

    @pl.when(i == 0)
    def _():
        for lb in range(QKV // LANE):
            stage[lb, 0:QKV_HALO, :] = halo_ref[:, lb * LANE:(lb + 1) * LANE]

    @pl.when(i > 0)
    def _():
        for lb in range(QKV // LANE):
            stage[lb, 0:QKV_HALO, :] = stage[lb, tl:tl + QKV_HALO, :]

    def project(cb):
        res = jnp.dot(u, wqkv_ref[:, cb * QKV_BLOCK:(cb + 1) * QKV_BLOCK], preferred_element_type=F32)
        for lbb in range(per_blk):
            stage[cb * per_blk + lbb, QKV_HALO:QKV_HALO + tl, :] = res[:, lbb * LANE:(lbb + 1) * LANE]

    def conv_act(cb):
        first = QKV_HALO - (DN_CONV_K - 1)
        for lb in range(cb * per_blk, (cb + 1) * per_blk):
            ls = slice(lb * LANE, (lb + 1) * LANE)
            for r0 in range(0, tl, SUBLANE * ROW_STRIDE):
                taps = [stage[lb, pl.ds(r0 + first + k, SUBLANE, stride=ROW_STRIDE), :]
                        for k in range(DN_CONV_K + ROW_STRIDE - 1)]
                for s in range(ROW_STRIDE):
                    acc = cw_ref[0:1, ls] * taps[s]
                    for j in range(1, DN_CONV_K):
                        acc = acc + cw_ref[j:j + 1, ls] * taps[j + s]
                    y = _silu(acc)
                    if lb < 2 * HEADS:
                        inv = lax.rsqrt(jnp.sum(y * y, axis=-1, keepdims=True) + EPS)
                        y = y * (inv * (DK ** -0.5) if lb < HEADS else inv)
                    qkv_ref[lb, pl.ds(r0 + s, SUBLANE, stride=ROW_STRIDE), :] = y

    project(0)
    for cb in range(nblk):
        if cb + 1 < nblk:
            project(cb + 1)
        else:
            cin = jnp.dot(u, wc_ref[...], preferred_element_type=F32)
            c_ref[...] = cin[:, :DC] * _sigmoid(cin[:, DC:])
        conv_act(cb)

    p = jnp.dot(u, wgb_ref[...], preferred_element_type=F32)
    g = -jnp.exp(alog_ref[...]) * _softplus(p + dtb_ref[...])
    lane = lax.broadcasted_iota(jnp.int32, p.shape, 1)
    val = jnp.where(lane < HEADS, g, jnp.where(lane < 2 * HEADS, _sigmoid(p), 0.0))
    if head_pass:
        val = jnp.where(_pad_row_mask(p.shape), val, 0.0)
    gb_ref[...] = val

    @pl.when(i == pl.num_programs(1) - 1)
    def _():
        for lb in range(QKV // LANE):
            tail_ref[:, lb * LANE:(lb + 1) * LANE] = stage[lb, tl:tl + QKV_HALO, :]


def _conf_kernel(c_ref, halo_ref, w_ref, b_ref, lnw_ref, lnb_ref, o_ref, cbuf, ybuf, *, tl):
    i = pl.program_id(1)
    nlb = DC // LANE

    @pl.when(i == 0)
    def _():
        for lb in range(nlb):
            cbuf[lb, 0:CONF_HALO, :] = halo_ref[:, lb * LANE:(lb + 1) * LANE]

    @pl.when(i > 0)
    def _():
        for lb in range(nlb):
            cbuf[lb, 0:CONF_HALO, :] = cbuf[lb, tl:tl + CONF_HALO, :]

    first = CONF_HALO - (CONF_K - 1)
    span = SUBLANE * ROW_STRIDE
    for lb in range(nlb):
        ls = slice(lb * LANE, (lb + 1) * LANE)
        cbuf[lb, CONF_HALO:CONF_HALO + tl, :] = c_ref[:, ls]
        for base in range(0, tl, span):
            taps = [cbuf[lb, pl.ds(base + first + k, SUBLANE, stride=ROW_STRIDE), :]
                    for k in range(CONF_K + ROW_STRIDE - 1)]
            ys = [jnp.broadcast_to(b_ref[:, ls], (SUBLANE, LANE))] * ROW_STRIDE
            for j in range(CONF_K):
                wj = w_ref[j:j + 1, ls]
                ys = [ys[s] + wj * taps[j + s] for s in range(ROW_STRIDE)]
            for s in range(ROW_STRIDE):
                ybuf[lb, pl.ds(base + s, SUBLANE, stride=ROW_STRIDE), :] = ys[s]
    y = jnp.concatenate([ybuf[lb] for lb in range(nlb)], axis=1)
    mu = jnp.mean(y, axis=-1, keepdims=True)
    yc = y - mu
    yn = yc * lax.rsqrt(jnp.mean(yc * yc, axis=-1, keepdims=True) + EPS)
    o_ref[...] = _silu(yn * lnw_ref[...] + lnb_ref[...]).astype(o_ref.dtype)


def _gdn_kernel(qkvs, h_ref, gb_ref, s0_ref, nw_ref, nmw_ref, wz_ref, tri_ref, masks_ref,
                o_ref, sout_ref, zbuf, wq_s, u_s, aqk_s, kdt_s, ecd_s, state, *, tl, group):
    i = pl.program_id(1)
    nc = tl // CHUNK

    @pl.when(i == 0)
    def _():
        state[...] = s0_ref[...]

    zbuf[...] = _silu(jnp.dot(_rmsnorm(h_ref[...], nmw_ref[...]).astype(BF16), wz_ref[...],
                              preferred_element_type=F32))

    lane = lax.broadcasted_iota(jnp.int32, (CHUNK, LANE), 1)
    ri = lax.broadcasted_iota(jnp.int32, (CHUNK, LANE), 0)
    ci = jnp.bitwise_and(lane, CHUNK - 1)
    lo = lane < CHUNK
    lo2 = lax.broadcasted_iota(jnp.int32, (2 * CHUNK, LANE), 1) < CHUNK

    def block_diag(m):
        keep = lo if m.shape[0] == CHUNK else lo2
        zero = jnp.zeros_like(m)
        return jnp.concatenate([jnp.where(keep, m, zero), jnp.where(keep, zero, m)], axis=0)

    def block_diag_wide(a, b):
        za = jnp.zeros_like(a)
        return jnp.concatenate([jnp.concatenate([a, za], axis=1), jnp.concatenate([za, b], axis=1)],
                               axis=0)

    def lane_pair(col0, col1):
        r = col0.shape[0]
        return jnp.concatenate([jnp.broadcast_to(col0, (r, LANE)), jnp.broadcast_to(col1, (r, LANE))],
                               axis=1)

    def prep_body(g, carry):
        probs = [(j, p) for j in range(group) for p in range(HEADS // 2)]
        incl = ri >= ci
        strict = ri > ci
        eye = jnp.where(ri == ci, 1.0, 0.0)
        tri = tri_ref[...]
        gbc, gc, gst, eg, ekd_t = [], [], [], [], []
        for j in range(group):
            rs = pl.ds(pl.multiple_of((g * group + j) * CHUNK, CHUNK), CHUNK)
            gb_j = gb_ref[rs, :]
            gc_j = sum(jnp.dot(tri, p, preferred_element_type=F32) for p in _split3(gb_j))
            gst_j = jnp.concatenate([gc_j, pltpu.roll(gc_j, LANE - 1, axis=1)], axis=0).T
            glast = jnp.where(lo2, gst_j[:, CHUNK - 1:CHUNK], gst_j[:, LANE - 1:LANE])
            gbc.append(gb_j)
            gc.append(gc_j)
            gst.append(gst_j)
            eg.append(jnp.exp(gc_j))
            ekd_t.append(jnp.exp(glast - gst_j))
            ecd_s[g * group + j] = jnp.exp(gc_j[CHUNK - 1:CHUNK, :])

        def rows_of(j):
            return pl.ds(pl.multiple_of((g * group + j) * CHUNK, CHUNK), CHUNK)

        def head_pair(pr, first_head):
            rs, h0 = rows_of(pr[0]), first_head + 2 * pr[1]
            return jnp.concatenate([qkvs[h0, rs, :], qkvs[h0 + 1, rs, :]], axis=1)

        q2 = {pr: head_pair(pr, 0) for pr in probs}
        k2 = {pr: head_pair(pr, HEADS) for pr in probs}
        v2 = {pr: head_pair(pr, 2 * HEADS) for pr in probs}
        k2t = {pr: jnp.concatenate([k2[pr][:, :DK], k2[pr][:, DK:]], axis=0).T for pr in probs}
        b2 = {pr: lane_pair(gbc[pr[0]][:, HEADS + 2 * pr[1]:HEADS + 2 * pr[1] + 1],
                            gbc[pr[0]][:, HEADS + 2 * pr[1] + 1:HEADS + 2 * pr[1] + 2]) for pr in probs}
        eg2 = {pr: lane_pair(eg[pr[0]][:, 2 * pr[1]:2 * pr[1] + 1],
                             eg[pr[0]][:, 2 * pr[1] + 1:2 * pr[1] + 2]) for pr in probs}
        kb2 = {pr: k2[pr] * b2[pr] for pr in probs}
        kq = {pr: _bdot(jnp.concatenate([kb2[pr], q2[pr]], axis=0), block_diag(k2t[pr].astype(BF16)))
              for pr in probs}
        decay = {}
        for pr in probs:
            j, p = pr
            gcol = jnp.where(lo, gc[j][:, 2 * p:2 * p + 1], gc[j][:, 2 * p + 1:2 * p + 2])
            decay[pr] = jnp.exp(jnp.where(incl, gcol - gst[j][2 * p:2 * p + 1, :], -jnp.inf))
        a_kk = {pr: jnp.where(strict, kq[pr][:CHUNK] * decay[pr], 0.0) for pr in probs}
        for pr in probs:
            j, p = pr
            aqk_s[g * group + j, p] = jnp.where(incl, kq[pr][CHUNK:] * decay[pr], 0.0).astype(BF16)
            kdt_s[g * group + j, p] = (k2t[pr] * ekd_t[j][2 * p:2 * p + 1, :]).astype(BF16)
        t = {pr: eye - a_kk[pr] * masks_ref[0] for pr in probs}
        for lvl in range(1, N_LEVELS):
            tb = {pr: t[pr].astype(BF16) for pr in probs}
            x = {pr: jnp.dot((a_kk[pr] * masks_ref[lvl]).astype(BF16), block_diag(tb[pr]),
                             preferred_element_type=F32) for pr in probs}
            t = {pr: t[pr] - jnp.dot(tb[pr], block_diag(x[pr].astype(BF16)),
                                     preferred_element_type=F32) for pr in probs}
        for pr in probs:
            j, p = pr
            rv = (v2[pr] * b2[pr]).astype(BF16)
            rk = (kb2[pr] * eg2[pr]).astype(BF16)
            rhs = block_diag_wide(jnp.concatenate([rv[:, :DV], rk[:, :DK]], axis=1),
                                  jnp.concatenate([rv[:, DV:], rk[:, DK:]], axis=1))
            sol = jnp.dot(t[pr].astype(BF16), rhs, preferred_element_type=F32)
            u_s[g * group + j, p] = jnp.concatenate([sol[:, :DV], sol[:, DV + DK:2 * DV + DK]], axis=1)
            w2 = jnp.concatenate([sol[:, DV:DV + DK], sol[:, 2 * DV + DK:]], axis=1)
            wq_s[g * group + j, p] = jnp.concatenate([w2, q2[pr] * eg2[pr]], axis=0).astype(BF16)
        return carry

    lax.fori_loop(0, nc // group, prep_body, 0)

    def scan_body(c, carry):
        ps = range(HEADS // 2)
        rs = pl.ds(c * CHUNK, CHUNK)
        ecd = ecd_s[c]
        s = [state[p] for p in ps]
        sb = [s[p].astype(BF16) for p in ps]
        ws_qs = [jnp.dot(wq_s[c, p], block_diag_wide(sb[p][:, :DV], sb[p][:, DV:]),
                         preferred_element_type=F32) for p in ps]
        vb = [(u_s[c, p] - ws_qs[p][:CHUNK]).astype(BF16) for p in ps]
        bdv = [block_diag_wide(vb[p][:, :DV], vb[p][:, DV:]) for p in ps]
        o = [ws_qs[p][CHUNK:] + jnp.dot(aqk_s[c, p], bdv[p], preferred_element_type=F32) for p in ps]
        for p in ps:
            cd = lane_pair(ecd[:, 2 * p:2 * p + 1], ecd[:, 2 * p + 1:2 * p + 2])
            state[p] = s[p] * cd + jnp.dot(kdt_s[c, p], bdv[p], preferred_element_type=F32)
        for p in ps:
            for half in range(2):
                h = 2 * p + half
                on = _rmsnorm(o[p][:, half * DV:(half + 1) * DV], nw_ref[...])
                o_ref[rs, h * DV:(h + 1) * DV] = (on * zbuf[rs, h * DV:(h + 1) * DV]).astype(o_ref.dtype)
        return carry

    for c in range(nc):
        scan_body(c, 0)

    @pl.when(i == pl.num_programs(1) - 1)
    def _():
        sout_ref[...] = state[...]


def _ffn_kernel(h_ref, c2_ref, o2_ref, hg_in_ref, hv_in_ref, nmw_ref, wgate_ref, bgate_ref,
                wco_ref, wdo_ref, wout_ref, nfw_ref, wupg_ref, wupv_ref, dwg_ref, dwv_ref,
                dbg_ref, dbv_ref, wdown_ref, nlw_ref, out_ref, hg_out_ref, hv_out_ref,
                gbuf0, vbuf0, gbuf1, vbuf1, actbuf, halo_g, halo_v, *, tl, head_pass):
    i = pl.program_id(1)

    @pl.when(i == 0)
    def _():
        halo_g[...] = hg_in_ref[...]
        halo_v[...] = hv_in_ref[...]

    h = h_ref[...]
    gates = _sigmoid(jnp.dot(_rmsnorm(h, nmw_ref[...]).astype(BF16), wgate_ref[...],
                             preferred_element_type=F32) + bgate_ref[...])
    yc = jnp.dot(c2_ref[...], wco_ref[...], preferred_element_type=F32)
    yd = jnp.dot(o2_ref[...], wdo_ref[...], preferred_element_type=F32)
    mix = gates[:, :D_MODEL] * yc + gates[:, D_MODEL:] * yd
    h1 = h + _bdot(mix, wout_ref[...])
    if head_pass:
        h1 = jnp.where(_pad_row_mask(h1.shape), h1, 0.0)
    u2 = _rmsnorm(h1, nfw_ref[...]).astype(BF16)

    first = FFN_HALO - (FFN_CONV_K - 1)
    bounds = [sum(FF_BLOCKS[:k]) for k in range(len(FF_BLOCKS) + 1)]
    bufs = ((gbuf0, vbuf0), (gbuf1, vbuf1))

    def up_project(k):
        cs = slice(bounds[k], bounds[k + 1])
        for buf, halo, wup in ((bufs[k % 2][0], halo_g, wupg_ref), (bufs[k % 2][1], halo_v, wupv_ref)):
            res = jnp.dot(u2, wup[:, cs], preferred_element_type=F32)
            for lbb in range(FF_BLOCKS[k] // LANE):
                hs = slice(bounds[k] + lbb * LANE, bounds[k] + (lbb + 1) * LANE)
                buf[lbb, 0:FFN_HALO, :] = halo[:, hs]
                buf[lbb, FFN_HALO:FFN_HALO + tl, :] = res[:, lbb * LANE:(lbb + 1) * LANE]
                halo[:, hs] = buf[lbb, tl:tl + FFN_HALO, :]

    def conv_glu(k):
        nlb = FF_BLOCKS[k] // LANE
        for lbb in range(nlb):
            hs = slice(bounds[k] + lbb * LANE, bounds[k] + (lbb + 1) * LANE)
            for r0 in range(0, tl, SUBLANE * ROW_STRIDE):
                ys = []
                for buf, dw, db in ((bufs[k % 2][0], dwg_ref, dbg_ref), (bufs[k % 2][1], dwv_ref, dbv_ref)):
                    taps = [buf[lbb, pl.ds(r0 + first + t, SUBLANE, stride=ROW_STRIDE), :]
                            for t in range(FFN_CONV_K + ROW_STRIDE - 1)]
                    ys.append([db[:, hs] + sum(dw[j:j + 1, hs] * taps[j + s] for j in range(FFN_CONV_K))
                               for s in range(ROW_STRIDE)])
                for s in range(ROW_STRIDE):
                    actbuf[lbb, pl.ds(r0 + s, SUBLANE, stride=ROW_STRIDE), :] = _silu(ys[0][s]) * ys[1][s]
        return jnp.concatenate([actbuf[lbb] for lbb in range(nlb)], axis=1).astype(BF16)

    acc = h1
    up_project(0)
    for k in range(len(FF_BLOCKS)):
        if k + 1 < len(FF_BLOCKS):
            up_project(k + 1)
        acc = acc + jnp.dot(conv_glu(k), wdown_ref[bounds[k]:bounds[k + 1], :],
                            preferred_element_type=F32)
    out_ref[...] = _rmsnorm(acc, nlw_ref[...])

    @pl.when(i == pl.num_programs(1) - 1)
    def _():
        hg_out_ref[...] = halo_g[...]
        hv_out_ref[...] = halo_v[...]


def _const_spec(shape):
    nd = len(shape)
    return pl.BlockSpec(shape, lambda b, i: (0,) * nd, pipeline_mode=pl.Buffered(1))


def _params(sem):
    return pltpu.CompilerParams(dimension_semantics=sem, vmem_limit_bytes=VMEM_LIMIT)


def _level_masks():
    r = jnp.arange(CHUNK)[:, None]
    c = jnp.arange(CHUNK)[None, :]
    masks = []
    for lvl in range(N_LEVELS):
        s = 1 << lvl
        masks.append(((r // (2 * s)) == (c // (2 * s))) & ((r // s) % 2 == 1) & ((c // s) % 2 == 0))
    return jnp.tile(jnp.stack(masks).astype(F32), (1, 1, LANE // CHUNK))


def _layer(h, carries, wts, *, tl, head_pass):
    bsz, length, _ = h.shape
    assert length % tl == 0
    grid = (bsz, length // tl)
    conf_halo, qkv_halo, s0, ffn_halo_g, ffn_halo_v = carries
    tile = lambda width: pl.BlockSpec((None, tl, width), lambda b, i: (b, i, 0))
    heads_tile = pl.BlockSpec((None, QKV // LANE, tl, LANE), lambda b, i: (b, 0, i, 0))
    cs = lambda a: _const_spec(a.shape)
    sds = jax.ShapeDtypeStruct
    tag = "head" if head_pass else "body"

    c, qkv, gb, qkv_tail = pl.pallas_call(
        functools.partial(_inproj_kernel, tl=tl, head_pass=head_pass),
        grid=grid,
        in_specs=[tile(D_MODEL), cs(qkv_halo), cs(wts["nmw"]), cs(wts["w_c"]), cs(wts["w_qkv"]),
                  cs(wts["w_gb"]), cs(wts["alog"]), cs(wts["dtb"]), cs(wts["dn_cw"])],
        out_specs=[tile(DC), heads_tile, tile(LANE),
                   pl.BlockSpec((None, QKV_HALO, QKV), lambda b, i: (b, 0, 0))],
        out_shape=[sds((bsz, length, DC), F32), sds((bsz, QKV // LANE, length, LANE), F32),
                   sds((bsz, length, LANE), F32), sds((bsz, QKV_HALO, QKV), F32)],
        scratch_shapes=[pltpu.VMEM((QKV // LANE, tl + QKV_HALO, LANE), F32)],
        compiler_params=_params(("arbitrary", "arbitrary")),
        name="inproj_" + tag,
    )(h, qkv_halo, wts["nmw"], wts["w_c"], wts["w_qkv"], wts["w_gb"], wts["alog"], wts["dtb"],
      wts["dn_cw"])

    c2 = pl.pallas_call(
        functools.partial(_conf_kernel, tl=tl),
        grid=grid,
        in_specs=[tile(DC), cs(conf_halo), cs(wts["conf_w"]), cs(wts["conf_b"]),
                  cs(wts["ln_w"]), cs(wts["ln_b"])],
        out_specs=tile(DC),
        out_shape=sds((bsz, length, DC), BF16),
        scratch_shapes=[pltpu.VMEM((DC // LANE, tl + CONF_HALO, LANE), F32),
                        pltpu.VMEM((DC // LANE, tl, LANE), F32)],
        compiler_params=_params(("arbitrary", "arbitrary")),
        name="conf_conv_" + tag,
    )(c, conf_halo, wts["conf_w"], wts["conf_b"], wts["ln_w"], wts["ln_b"])

    nc = tl // CHUNK
    group = min(CHUNK_GROUP, nc)
    o2, s_out = pl.pallas_call(
        functools.partial(_gdn_kernel, tl=tl, group=group),
        grid=grid,
        in_specs=[heads_tile, tile(D_MODEL), tile(LANE), cs(s0),
                  cs(wts["dn_nw"]), cs(wts["nmw"]), cs(wts["w_z"]), cs(wts["tri"]), cs(wts["masks"])],
        out_specs=[tile(HV), pl.BlockSpec((None,) + STATE_SHAPE, lambda b, i: (b, 0, 0, 0))],
        out_shape=[sds((bsz, length, HV), BF16), sds((bsz,) + STATE_SHAPE, F32)],
        scratch_shapes=[pltpu.VMEM((tl, HV), F32),
                        pltpu.VMEM((nc, HEADS // 2, 2 * CHUNK, 2 * DK), BF16),
                        pltpu.VMEM((nc, HEADS // 2, CHUNK, 2 * DV), F32),
                        pltpu.VMEM((nc, HEADS // 2, CHUNK, LANE), BF16),
                        pltpu.VMEM((nc, HEADS // 2, DK, LANE), BF16),
                        pltpu.VMEM((nc, 1, LANE), F32),
                        pltpu.VMEM(STATE_SHAPE, F32)],
        compiler_params=_params(("arbitrary", "arbitrary")),
        name="gdn_" + tag,
    )(qkv, h, gb, s0, wts["dn_nw"], wts["nmw"], wts["w_z"], wts["tri"], wts["masks"])

    halo_spec = pl.BlockSpec((None, FFN_HALO, D_FF), lambda b, i: (b, 0, 0))
    out, hg, hv = pl.pallas_call(
        functools.partial(_ffn_kernel, tl=tl, head_pass=head_pass),
        grid=grid,
        in_specs=[tile(D_MODEL), tile(DC), tile(HV), cs(ffn_halo_g), cs(ffn_halo_v),
                  cs(wts["nmw"]), cs(wts["w_gate"]), cs(wts["b_gate"]), cs(wts["w_co"]),
                  cs(wts["w_do"]), cs(wts["w_out"]), cs(wts["nfw"]), cs(wts["w_upg"]),
                  cs(wts["w_upv"]), cs(wts["dw_g"]), cs(wts["dw_v"]), cs(wts["db_g"]),
                  cs(wts["db_v"]), cs(wts["w_down"]), cs(wts["nlw"])],
        out_specs=[tile(D_MODEL), halo_spec, halo_spec],
        out_shape=[sds((bsz, length, D_MODEL), F32), sds((bsz, FFN_HALO, D_FF), F32),
                   sds((bsz, FFN_HALO, D_FF), F32)],
        scratch_shapes=[pltpu.VMEM((max(FF_BLOCKS) // LANE, tl + FFN_HALO, LANE), F32)] * 4
        + [pltpu.VMEM((max(FF_BLOCKS) // LANE, tl, LANE), F32)]
        + [pltpu.VMEM((FFN_HALO, D_FF), F32)] * 2,
        compiler_params=_params(("arbitrary", "arbitrary")),
        name="merge_ffn_" + tag,
    )(h, c2, o2, ffn_halo_g, ffn_halo_v, wts["nmw"], wts["w_gate"], wts["b_gate"], wts["w_co"],
      wts["w_do"], wts["w_out"], wts["nfw"], wts["w_upg"], wts["w_upv"], wts["dw_g"], wts["dw_v"],
      wts["db_g"], wts["db_v"], wts["w_down"], wts["nlw"])

    new_carries = (c[0, length - CONF_HALO:], qkv_tail[0], s_out[0], hg[0], hv[0])
    return out, new_carries


def kernel(x, meta_tokens, norm_mix_w, w_in, b_gate, conf_dw_w, conf_dw_b, conf_ln_w, conf_ln_b, w_conf_out, dn_conv_w, dn_A_log, dn_dt_bias, dn_norm_w, w_dn_out, w_out, norm_ffn_w, w_up, ffn_dw_w, ffn_dw_b, w_down, norm_final_w):
    bsz, seq, _ = x.shape
    assert seq % BODY_ROWS == 0, "sequence must be a whole number of body tiles"
    assert w_in.shape[0] == 1, "single-layer block"

    wi = w_in[0]
    o_q = 2 * DC
    o_z = o_q + QKV
    o_a = o_z + HV
    o_gate = o_a + 2 * HEADS
    row = lambda v: v.reshape(1, -1).astype(F32)
    lane_pad = lambda v: jnp.pad(row(v), ((0, 0), (0, LANE - v.shape[-1])))
    wu = w_up[0]
    wts = {
        "nmw": row(norm_mix_w[0]),
        "w_c": wi[:, :o_q].astype(BF16),
        "w_qkv": wi[:, o_q:o_z].astype(BF16),
        "w_z": wi[:, o_z:o_a].astype(BF16),
        "w_gb": jnp.pad(wi[:, o_a:o_gate], ((0, 0), (0, LANE - 2 * HEADS))).astype(BF16),
        "w_gate": wi[:, o_gate:].astype(BF16),
        "b_gate": row(b_gate[0]),
        "alog": lane_pad(dn_A_log[0]),
        "dtb": lane_pad(dn_dt_bias[0]),
        "conf_w": conf_dw_w[0].astype(F32),
        "conf_b": row(conf_dw_b[0]),
        "ln_w": row(conf_ln_w[0]),
        "ln_b": row(conf_ln_b[0]),
        "dn_cw": dn_conv_w[0].astype(F32),
        "dn_nw": row(dn_norm_w[0]),
        "tri": (jnp.arange(CHUNK)[:, None] >= jnp.arange(CHUNK)[None, :]).astype(BF16),
        "masks": _level_masks(),
        "w_co": w_conf_out[0].astype(BF16),
        "w_do": w_dn_out[0].astype(BF16),
        "w_out": w_out[0].astype(BF16),
        "nfw": row(norm_ffn_w[0]),
        "w_upg": wu[:, :D_FF].astype(BF16),
        "w_upv": wu[:, D_FF:].astype(BF16),
        "dw_g": ffn_dw_w[0][:, :D_FF].astype(F32),
        "dw_v": ffn_dw_w[0][:, D_FF:].astype(F32),
        "db_g": row(ffn_dw_b[0][:D_FF]),
        "db_v": row(ffn_dw_b[0][D_FF:]),
        "w_down": w_down[0].astype(BF16),
        "nlw": row(norm_final_w),
    }

    zero_carries = (jnp.zeros((CONF_HALO, DC), F32), jnp.zeros((QKV_HALO, QKV), F32),
                    jnp.zeros(STATE_SHAPE, F32), jnp.zeros((FFN_HALO, D_FF), F32),
                    jnp.zeros((FFN_HALO, D_FF), F32))
    head = jnp.concatenate([jnp.zeros((PAD, D_MODEL), x.dtype), meta_tokens.astype(x.dtype)])[None]
    _, carries = _layer(head, zero_carries, wts, tl=CHUNK, head_pass=True)
    out, _ = _layer(x, carries, wts, tl=BODY_ROWS, head_pass=False)
    return out
```

```python
import functools

import jax
import jax.numpy as jnp
from jax import lax
from jax.experimental import pallas as pl
from jax.experimental.pallas import tpu as pltpu

D_MODEL = 1024
N_META = 16
CHUNK = 64
PAD = CHUNK - N_META
DC = D_MODEL
CONF_K = 31
HEADS = 8
DK = 128
DV = 128
HK = HEADS * DK
HV = HEADS * DV
QKV = 2 * HK + HV
DN_CONV_K = 4
D_FF = 2816
FFN_CONV_K = 3
EPS = 1e-6

LANE = 128
SUBLANE = 8
BODY_ROWS = 512
MXU_DIM = 256
FF_BLOCKS = (3 * MXU_DIM, 3 * MXU_DIM, 3 * MXU_DIM, 2 * MXU_DIM)
QKV_BLOCK = 2 * MXU_DIM
ROW_STRIDE = 2
CHUNK_GROUP = 8
VMEM_LIMIT = 56 * 1024 * 1024

CONF_HALO = 32
QKV_HALO = SUBLANE
FFN_HALO = SUBLANE
N_LEVELS = 6
STATE_SHAPE = (HEADS // 2, DK, 2 * DV)

F32 = jnp.float32
BF16 = jnp.bfloat16

assert sum(FF_BLOCKS) == D_FF


def _bdot(a, b):
    return jnp.dot(a.astype(BF16), b.astype(BF16), preferred_element_type=F32)


NEG_LOG2_E = -1.4426950408889634


def _sigmoid(x):
    return 1.0 / (1.0 + jnp.exp2(x * NEG_LOG2_E))


def _silu(x):
    return x * _sigmoid(x)


def _softplus(x):
    return jnp.maximum(x, 0.0) + jnp.log(1.0 + jnp.exp(-jnp.abs(x)))


def _rmsnorm(x, w):
    return x * lax.rsqrt(jnp.mean(x * x, axis=-1, keepdims=True) + EPS) * w


def _split3(x):
    hi = x.astype(BF16)
    r = x - hi.astype(F32)
    mid = r.astype(BF16)
    lo = (r - mid.astype(F32)).astype(BF16)
    return hi, mid, lo


def _pad_row_mask(shape):
    return lax.broadcasted_iota(jnp.int32, shape, 0) >= PAD


def _inproj_kernel(h_ref, halo_ref, nw_ref, wc_ref, wqkv_ref, wgb_ref, alog_ref, dtb_ref, cw_ref,
                   c_ref, qkv_ref, gb_ref, tail_ref, stage, *, tl, head_pass):
    i = pl.program_id(1)
    nblk = QKV // QKV_BLOCK
    per_blk = QKV_BLOCK // LANE
    u = _rmsnorm(h_ref[...], nw_ref[...]).astype(BF16)

    @pl.when(i == 0)
    def _():
        for lb in range(QKV // LANE):
            stage[lb, 0:QKV_HALO, :] = halo_ref[:, lb * LANE:(lb + 1) * LANE]

    @pl.when(i > 0)
    def _():
        for lb in range(QKV // LANE):
            stage[lb, 0:QKV_HALO, :] = stage[lb, tl:tl + QKV_HALO, :]

    def project(cb):
        res = jnp.dot(u, wqkv_ref[:, cb * QKV_BLOCK:(cb + 1) * QKV_BLOCK], preferred_element_type=F32)
        for lbb in range(per_blk):
            stage[cb * per_blk + lbb, QKV_HALO:QKV_HALO + tl, :] = res[:, lbb * LANE:(lbb + 1) * LANE]

    def conv_act(cb):
        first = QKV_HALO - (DN_CONV_K - 1)
        for lb in range(cb * per_blk, (cb + 1) * per_blk):
            ls = slice(lb * LANE, (lb + 1) * LANE)
            for r0 in range(0, tl, SUBLANE * ROW_STRIDE):
                taps = [stage[lb, pl.ds(r0 + first + k, SUBLANE, stride=ROW_STRIDE), :]
                        for k in range(DN_CONV_K + ROW_STRIDE - 1)]
                for s in range(ROW_STRIDE):
                    acc = cw_ref[0:1, ls] * taps[s]
                    for j in range(1, DN_CONV_K):
                        acc = acc + cw_ref[j:j + 1, ls] * taps[j + s]
                    y = _silu(acc)
                    if lb < 2 * HEADS:
                        inv = lax.rsqrt(jnp.sum(y * y, axis=-1, keepdims=True) + EPS)
                        y = y * (inv * (DK ** -0.5) if lb < HEADS else inv)
                    qkv_ref[lb, pl.ds(r0 + s, SUBLANE, stride=ROW_STRIDE), :] = y

    project(0)
    for cb in range(nblk):
        if cb + 1 < nblk:
            project(cb + 1)
        else:
            cin = jnp.dot(u, wc_ref[...], preferred_element_type=F32)
            c_ref[...] = cin[:, :DC] * _sigmoid(cin[:, DC:])
        conv_act(cb)

    p = jnp.dot(u, wgb_ref[...], preferred_element_type=F32)
    g = -jnp.exp(alog_ref[...]) * _softplus(p + dtb_ref[...])
    lane = lax.broadcasted_iota(jnp.int32, p.shape, 1)
    val = jnp.where(lane < HEADS, g, jnp.where(lane < 2 * HEADS, _sigmoid(p), 0.0))
    if head_pass:
        val = jnp.where(_pad_row_mask(p.shape), val, 0.0)
    gb_ref[...] = val

    @pl.when(i == pl.num_programs(1) - 1)
    def _():
        for lb in range(QKV // LANE):
            tail_ref[:, lb * LANE:(lb + 1) * LANE] = stage[lb, tl:tl + QKV_HALO, :]


def _conf_kernel(c_ref, halo_ref, w_ref, b_ref, lnw_ref, lnb_ref, o_ref, cbuf, ybuf, *, tl):
    i = pl.program_id(1)
    nlb = DC // LANE

    @pl.when(i == 0)
    def _():
        for lb in range(nlb):
            cbuf[lb, 0:CONF_HALO, :] = halo_ref[:, lb * LANE:(lb + 1) * LANE]

    @pl.when(i > 0)
    def _():
        for lb in range(nlb):
            cbuf[lb, 0:CONF_HALO, :] = cbuf[lb, tl:tl + CONF_HALO, :]

    first = CONF_HALO - (CONF_K - 1)
    span = SUBLANE * ROW_STRIDE
    for lb in range(nlb):
        ls = slice(lb * LANE, (lb + 1) * LANE)
        cbuf[lb, CONF_HALO:CONF_HALO + tl, :] = c_ref[:, ls]
        for base in range(0, tl, span):
            taps = [cbuf[lb, pl.ds(base + first + k, SUBLANE, stride=ROW_STRIDE), :]
                    for k in range(CONF_K + ROW_STRIDE - 1)]
            ys = [jnp.broadcast_to(b_ref[:, ls], (SUBLANE, LANE))] * ROW_STRIDE
            for j in range(CONF_K):
                wj = w_ref[j:j + 1, ls]
                ys = [ys[s] + wj * taps[j + s] for s in range(ROW_STRIDE)]
            for s in range(ROW_STRIDE):
                ybuf[lb, pl.ds(base + s, SUBLANE, stride=ROW_STRIDE), :] = ys[s]
    y = jnp.concatenate([ybuf[lb] for lb in range(nlb)], axis=1)
    mu = jnp.mean(y, axis=-1, keepdims=True)
    yc = y - mu
    yn = yc * lax.rsqrt(jnp.mean(yc * yc, axis=-1, keepdims=True) + EPS)
    o_ref[...] = _silu(yn * lnw_ref[...] + lnb_ref[...]).astype(o_ref.dtype)


def _gdn_kernel(qkvs, h_ref, gb_ref, s0_ref, nw_ref, nmw_ref, wz_ref, tri_ref, masks_ref,
                o_ref, sout_ref, zbuf, wq_s, u_s, aqk_s, kdt_s, ecd_s, state, *, tl, group):
    i = pl.program_id(1)
    nc = tl // CHUNK

    @pl.when(i == 0)
    def _():
        state[...] = s0_ref[...]

    zbuf[...] = _silu(jnp.dot(_rmsnorm(h_ref[...], nmw_ref[...]).astype(BF16), wz_ref[...],
                              preferred_element_type=F32))

    lane = lax.broadcasted_iota(jnp.int32, (CHUNK, LANE), 1)
    ri = lax.broadcasted_iota(jnp.int32, (CHUNK, LANE), 0)
    ci = jnp.bitwise_and(lane, CHUNK - 1)
    lo = lane < CHUNK
    lo2 = lax.broadcasted_iota(jnp.int32, (2 * CHUNK, LANE), 1) < CHUNK

    def block_diag(m):
        keep = lo if m.shape[0] == CHUNK else lo2
        zero = jnp.zeros_like(m)
        return jnp.concatenate([jnp.where(keep, m, zero), jnp.where(keep, zero, m)], axis=0)

    def block_diag_wide(a, b):
        za = jnp.zeros_like(a)
        return jnp.concatenate([jnp.concatenate([a, za], axis=1), jnp.concatenate([za, b], axis=1)],
                               axis=0)

    def lane_pair(col0, col1):
        r = col0.shape[0]
        return jnp.concatenate([jnp.broadcast_to(col0, (r, LANE)), jnp.broadcast_to(col1, (r, LANE))],
                               axis=1)

    def prep_body(g, carry):
        probs = [(j, p) for j in range(group) for p in range(HEADS // 2)]
        incl = ri >= ci
        strict = ri > ci
        eye = jnp.where(ri == ci, 1.0, 0.0)
        tri = tri_ref[...]
        gbc, gc, gst, eg, ekd_t = [], [], [], [], []
        for j in range(group):
            rs = pl.ds(pl.multiple_of((g * group + j) * CHUNK, CHUNK), CHUNK)
            gb_j = gb_ref[rs, :]
            gc_j = sum(jnp.dot(tri, p, preferred_element_type=F32) for p in _split3(gb_j))
            gst_j = jnp.concatenate([gc_j, pltpu.roll(gc_j, LANE - 1, axis=1)], axis=0).T
            glast = jnp.where(lo2, gst_j[:, CHUNK - 1:CHUNK], gst_j[:, LANE - 1:LANE])
            gbc.append(gb_j)
            gc.append(gc_j)
            gst.append(gst_j)
            eg.append(jnp.exp(gc_j))
            ekd_t.append(jnp.exp(glast - gst_j))
            ecd_s[g * group + j] = jnp.exp(gc_j[CHUNK - 1:CHUNK, :])

        def rows_of(j):
            return pl.ds(pl.multiple_of((g * group + j) * CHUNK, CHUNK), CHUNK)

        def head_pair(pr, first_head):
            rs, h0 = rows_of(pr[0]), first_head + 2 * pr[1]
            return jnp.concatenate([qkvs[h0, rs, :], qkvs[h0 + 1, rs, :]], axis=1)

        q2 = {pr: head_pair(pr, 0) for pr in probs}
        k2 = {pr: head_pair(pr, HEADS) for pr in probs}
        v2 = {pr: head_pair(pr, 2 * HEADS) for pr in probs}
        k2t = {pr: jnp.concatenate([k2[pr][:, :DK], k2[pr][:, DK:]], axis=0).T for pr in probs}
        b2 = {pr: lane_pair(gbc[pr[0]][:, HEADS + 2 * pr[1]:HEADS + 2 * pr[1] + 1],
                            gbc[pr[0]][:, HEADS + 2 * pr[1] + 1:HEADS + 2 * pr[1] + 2]) for pr in probs}
        eg2 = {pr: lane_pair(eg[pr[0]][:, 2 * pr[1]:2 * pr[1] + 1],
                             eg[pr[0]][:, 2 * pr[1] + 1:2 * pr[1] + 2]) for pr in probs}
        kb2 = {pr: k2[pr] * b2[pr] for pr in probs}
        kq = {pr: _bdot(jnp.concatenate([kb2[pr], q2[pr]], axis=0), block_diag(k2t[pr].astype(BF16)))
              for pr in probs}
        decay = {}
        for pr in probs:
            j, p = pr
            gcol = jnp.where(lo, gc[j][:, 2 * p:2 * p + 1], gc[j][:, 2 * p + 1:2 * p + 2])
            decay[pr] = jnp.exp(jnp.where(incl, gcol - gst[j][2 * p:2 * p + 1, :], -jnp.inf))
        a_kk = {pr: jnp.where(strict, kq[pr][:CHUNK] * decay[pr], 0.0) for pr in probs}
        for pr in probs:
            j, p = pr
            aqk_s[g * group + j, p] = jnp.where(incl, kq[pr][CHUNK:] * decay[pr], 0.0).astype(BF16)
            kdt_s[g * group + j, p] = (k2t[pr] * ekd_t[j][2 * p:2 * p + 1, :]).astype(BF16)
        t = {pr: eye - a_kk[pr] * masks_ref[0] for pr in probs}
        for lvl in range(1, N_LEVELS):
            tb = {pr: t[pr].astype(BF16) for pr in probs}
            x = {pr: jnp.dot((a_kk[pr] * masks_ref[lvl]).astype(BF16), block_diag(tb[pr]),
                             preferred_element_type=F32) for pr in probs}
            t = {pr: t[pr] - jnp.dot(tb[pr], block_diag(x[pr].astype(BF16)),
                                     preferred_element_type=F32) for pr in probs}
        for pr in probs:
            j, p = pr
            rv = (v2[pr] * b2[pr]).astype(BF16)
            rk = (kb2[pr] * eg2[pr]).astype(BF16)
            rhs = block_diag_wide(jnp.concatenate([rv[:, :DV], rk[:, :DK]], axis=1),
                                  jnp.concatenate([rv[:, DV:], rk[:, DK:]], axis=1))
            sol = jnp.dot(t[pr].astype(BF16), rhs, preferred_element_type=F32)
            u_s[g * group + j, p] = jnp.concatenate([sol[:, :DV], sol[:, DV + DK:2 * DV + DK]], axis=1)
            w2 = jnp.concatenate([sol[:, DV:DV + DK], sol[:, 2 * DV + DK:]], axis=1)
            wq_s[g * group + j, p] = jnp.concatenate([w2, q2[pr] * eg2[pr]], axis=0).astype(BF16)
        return carry

    lax.fori_loop(0, nc // group, prep_body, 0)

    def scan_body(c, carry):
        ps = range(HEADS // 2)
        rs = pl.ds(c * CHUNK, CHUNK)
        ecd = ecd_s[c]
        s = [state[p] for p in ps]
        sb = [s[p].astype(BF16) for p in ps]
        ws_qs = [jnp.dot(wq_s[c, p], block_diag_wide(sb[p][:, :DV], sb[p][:, DV:]),
                         preferred_element_type=F32) for p in ps]
        vb = [(u_s[c, p] - ws_qs[p][:CHUNK]).astype(BF16) for p in ps]
        bdv = [block_diag_wide(vb[p][:, :DV], vb[p][:, DV:]) for p in ps]
        o = [ws_qs[p][CHUNK:] + jnp.dot(aqk_s[c, p], bdv[p], preferred_element_type=F32) for p in ps]
        for p in ps:
            cd = lane_pair(ecd[:, 2 * p:2 * p + 1], ecd[:, 2 * p + 1:2 * p + 2])
            state[p] = s[p] * cd + jnp.dot(kdt_s[c, p], bdv[p], preferred_element_type=F32)
        for p in ps:
            for half in range(2):
                h = 2 * p + half
                on = _rmsnorm(o[p][:, half * DV:(half + 1) * DV], nw_ref[...])
                o_ref[rs, h * DV:(h + 1) * DV] = (on * zbuf[rs, h * DV:(h + 1) * DV]).astype(o_ref.dtype)
        return carry

    for c in range(nc):
        scan_body(c, 0)

    @pl.when(i == pl.num_programs(1) - 1)
    def _():
        sout_ref[...] = state[...]


def _ffn_kernel(h_ref, c2_ref, o2_ref, hg_in_ref, hv_in_ref, nmw_ref, wgate_ref, bgate_ref,
                wco_ref, wdo_ref, wout_ref, nfw_ref, wupg_ref, wupv_ref, dwg_ref, dwv_ref,
                dbg_ref, dbv_ref, wdown_ref, nlw_ref, out_ref, hg_out_ref, hv_out_ref,
                gbuf0, vbuf0, gbuf1, vbuf1, actbuf, halo_g, halo_v, *, tl, head_pass):
    i = pl.program_id(1)

    @pl.when(i == 0)
    def _():
        halo_g[...] = hg_in_ref[...]
        halo_v[...] = hv_in_ref[...]

    h = h_ref[...]
    gates = _sigmoid(jnp.dot(_rmsnorm(h, nmw_ref[...]).astype(BF16), wgate_ref[...],
                             preferred_element_type=F32) + bgate_ref[...])
    yc = jnp.dot(c2_ref[...], wco_ref[...], preferred_element_type=F32)
    yd = jnp.dot(o2_ref[...], wdo_ref[...], preferred_element_type=F32)
    mix = gates[:, :D_MODEL] * yc + gates[:, D_MODEL:] * yd
    h1 = h + _bdot(mix, wout_ref[...])
    if head_pass:
        h1 = jnp.where(_pad_row_mask(h1.shape), h1, 0.0)
    u2 = _rmsnorm(h1, nfw_ref[...]).astype(BF16)

    first = FFN_HALO - (FFN_CONV_K - 1)
    bounds = [sum(FF_BLOCKS[:k]) for k in range(len(FF_BLOCKS) + 1)]
    bufs = ((gbuf0, vbuf0), (gbuf1, vbuf1))

    def up_project(k):
        cs = slice(bounds[k], bounds[k + 1])
        for buf, halo, wup in ((bufs[k % 2][0], halo_g, wupg_ref), (bufs[k % 2][1], halo_v, wupv_ref)):
            res = jnp.dot(u2, wup[:, cs], preferred_element_type=F32)
            for lbb in range(FF_BLOCKS[k] // LANE):
                hs = slice(bounds[k] + lbb * LANE, bounds[k] + (lbb + 1) * LANE)
                buf[lbb, 0:FFN_HALO, :] = halo[:, hs]
                buf[lbb, FFN_HALO:FFN_HALO + tl, :] = res[:, lbb * LANE:(lbb + 1) * LANE]
                halo[:, hs] = buf[lbb, tl:tl + FFN_HALO, :]

    def conv_glu(k):
        nlb = FF_BLOCKS[k] // LANE
        for lbb in range(nlb):
            hs = slice(bounds[k] + lbb * LANE, bounds[k] + (lbb + 1) * LANE)
            for r0 in range(0, tl, SUBLANE * ROW_STRIDE):
                ys = []
                for buf, dw, db in ((bufs[k % 2][0], dwg_ref, dbg_ref), (bufs[k % 2][1], dwv_ref, dbv_ref)):
                    taps = [buf[lbb, pl.ds(r0 + first + t, SUBLANE, stride=ROW_STRIDE), :]
                            for t in range(FFN_CONV_K + ROW_STRIDE - 1)]
                    ys.append([db[:, hs] + sum(dw[j:j + 1, hs] * taps[j + s] for j in range(FFN_CONV_K))
                               for s in range(ROW_STRIDE)])
                for s in range(ROW_STRIDE):
                    actbuf[lbb, pl.ds(r0 + s, SUBLANE, stride=ROW_STRIDE), :] = _silu(ys[0][s]) * ys[1][s]
        return jnp.concatenate([actbuf[lbb] for lbb in range(nlb)], axis=1).astype(BF16)

    acc = h1
    up_project(0)
    for k in range(len(FF_BLOCKS)):
        if k + 1 < len(FF_BLOCKS):
            up_project(k + 1)
        acc = acc + jnp.dot(conv_glu(k), wdown_ref[bounds[k]:bounds[k + 1], :],
                            preferred_element_type=F32)
    out_ref[...] = _rmsnorm(acc, nlw_ref[...])

    @pl.when(i == pl.num_programs(1) - 1)
    def _():
        hg_out_ref[...] = halo_g[...]
        hv_out_ref[...] = halo_v[...]


def _const_spec(shape):
    nd = len(shape)
    return pl.BlockSpec(shape, lambda b, i: (0,) * nd, pipeline_mode=pl.Buffered(1))


def _params(sem):
    return pltpu.CompilerParams(dimension_semantics=sem, vmem_limit_bytes=VMEM_LIMIT)


def _level_masks():
    r = jnp.arange(CHUNK)[:, None]
    c = jnp.arange(CHUNK)[None, :]
    masks = []
    for lvl in range(N_LEVELS):
        s = 1 << lvl
        masks.append(((r // (2 * s)) == (c // (2 * s))) & ((r // s) % 2 == 1) & ((c // s) % 2 == 0))
    return jnp.tile(jnp.stack(masks).astype(F32), (1, 1, LANE // CHUNK))


def _layer(h, carries, wts, *, tl, head_pass):
    bsz, length, _ = h.shape
    assert length % tl == 0
    grid = (bsz, length // tl)
    conf_halo, qkv_halo, s0, ffn_halo_g, ffn_halo_v = carries
    tile = lambda width: pl.BlockSpec((None, tl, width), lambda b, i: (b, i, 0))
    heads_tile = pl.BlockSpec((None, QKV // LANE, tl, LANE), lambda b, i: (b, 0, i, 0))
    cs = lambda a: _const_spec(a.shape)
    sds = jax.ShapeDtypeStruct
    tag = "head" if head_pass else "body"

    c, qkv, gb, qkv_tail = pl.pallas_call(
        functools.partial(_inproj_kernel, tl=tl, head_pass=head_pass),
        grid=grid,
        in_specs=[tile(D_MODEL), cs(qkv_halo), cs(wts["nmw"]), cs(wts["w_c"]), cs(wts["w_qkv"]),
                  cs(wts["w_gb"]), cs(wts["alog"]), cs(wts["dtb"]), cs(wts["dn_cw"])],
        out_specs=[tile(DC), heads_tile, tile(LANE),
                   pl.BlockSpec((None, QKV_HALO, QKV), lambda b, i: (b, 0, 0))],
        out_shape=[sds((bsz, length, DC), F32), sds((bsz, QKV // LANE, length, LANE), F32),
                   sds((bsz, length, LANE), F32), sds((bsz, QKV_HALO, QKV), F32)],
        scratch_shapes=[pltpu.VMEM((QKV // LANE, tl + QKV_HALO, LANE), F32)],
        compiler_params=_params(("arbitrary", "arbitrary")),
        name="inproj_" + tag,
    )(h, qkv_halo, wts["nmw"], wts["w_c"], wts["w_qkv"], wts["w_gb"], wts["alog"], wts["dtb"],
      wts["dn_cw"])

    c2 = pl.pallas_call(
        functools.partial(_conf_kernel, tl=tl),
        grid=grid,
        in_specs=[tile(DC), cs(conf_halo), cs(wts["conf_w"]), cs(wts["conf_b"]),
                  cs(wts["ln_w"]), cs(wts["ln_b"])],
        out_specs=tile(DC),
        out_shape=sds((bsz, length, DC), BF16),
        scratch_shapes=[pltpu.VMEM((DC // LANE, tl + CONF_HALO, LANE), F32),
                        pltpu.VMEM((DC // LANE, tl, LANE), F32)],
        compiler_params=_params(("arbitrary", "arbitrary")),
        name="conf_conv_" + tag,
    )(c, conf_halo, wts["conf_w"], wts["conf_b"], wts["ln_w"], wts["ln_b"])

    nc = tl // CHUNK
    group = min(CHUNK_GROUP, nc)
    o2, s_out = pl.pallas_call(
        functools.partial(_gdn_kernel, tl=tl, group=group),
        grid=grid,
        in_specs=[heads_tile, tile(D_MODEL), tile(LANE), cs(s0),
                  cs(wts["dn_nw"]), cs(wts["nmw"]), cs(wts["w_z"]), cs(wts["tri"]), cs(wts["masks"])],
        out_specs=[tile(HV), pl.BlockSpec((None,) + STATE_SHAPE, lambda b, i: (b, 0, 0, 0))],
        out_shape=[sds((bsz, length, HV), BF16), sds((bsz,) + STATE_SHAPE, F32)],
        scratch_shapes=[pltpu.VMEM((tl, HV), F32),
                        pltpu.VMEM((nc, HEADS // 2, 2 * CHUNK, 2 * DK), BF16),
                        pltpu.VMEM((nc, HEADS // 2, CHUNK, 2 * DV), F32),
                        pltpu.VMEM((nc, HEADS // 2, CHUNK, LANE), BF16),
                        pltpu.VMEM((nc, HEADS // 2, DK, LANE), BF16),
                        pltpu.VMEM((nc, 1, LANE), F32),
                        pltpu.VMEM(STATE_SHAPE, F32)],
        compiler_params=_params(("arbitrary", "arbitrary")),
        name="gdn_" + tag,
    )(qkv, h, gb, s0, wts["dn_nw"], wts["nmw"], wts["w_z"], wts["tri"], wts["masks"])

    halo_spec = pl.BlockSpec((None, FFN_HALO, D_FF), lambda b, i: (b, 0, 0))
    out, hg, hv = pl.pallas_call(
        functools.partial(_ffn_kernel, tl=tl, head_pass=head_pass),
        grid=grid,
        in_specs=[tile(D_MODEL), tile(DC), tile(HV), cs(ffn_halo_g), cs(ffn_halo_v),
                  cs(wts["nmw"]), cs(wts["w_gate"]), cs(wts["b_gate"]), cs(wts["w_co"]),
                  cs(wts["w_do"]), cs(wts["w_out"]), cs(wts["nfw"]), cs(wts["w_upg"]),
                  cs(wts["w_upv"]), cs(wts["dw_g"]), cs(wts["dw_v"]), cs(wts["db_g"]),
                  cs(wts["db_v"]), cs(wts["w_down"]), cs(wts["nlw"])],
        out_specs=[tile(D_MODEL), halo_spec, halo_spec],
        out_shape=[sds((bsz, length, D_MODEL), F32), sds((bsz, FFN_HALO, D_FF), F32),
                   sds((bsz, FFN_HALO, D_FF), F32)],
        scratch_shapes=[pltpu.VMEM((max(FF_BLOCKS) // LANE, tl + FFN_HALO, LANE), F32)] * 4
        + [pltpu.VMEM((max(FF_BLOCKS) // LANE, tl, LANE), F32)]
        + [pltpu.VMEM((FFN_HALO, D_FF), F32)] * 2,
        compiler_params=_params(("arbitrary", "arbitrary")),
        name="merge_ffn_" + tag,
    )(h, c2, o2, ffn_halo_g, ffn_halo_v, wts["nmw"], wts["w_gate"], wts["b_gate"], wts["w_co"],
      wts["w_do"], wts["w_out"], wts["nfw"], wts["w_upg"], wts["w_upv"], wts["dw_g"], wts["dw_v"],
      wts["db_g"], wts["db_v"], wts["w_down"], wts["nlw"])

    new_carries = (c[0, length - CONF_HALO:], qkv_tail[0], s_out[0], hg[0], hv[0])
    return out, new_carries


def kernel(x, meta_tokens, norm_mix_w, w_in, b_gate, conf_dw_w, conf_dw_b, conf_ln_w, conf_ln_b, w_conf_out, dn_conv_w, dn_A_log, dn_dt_bias, dn_norm_w, w_dn_out, w_out, norm_ffn_w, w_up, ffn_dw_w, ffn_dw_b, w_down, norm_final_w):
    bsz, seq, _ = x.shape
    assert seq % BODY_ROWS == 0, "sequence must be a whole number of body tiles"
    assert w_in.shape[0] == 1, "single-layer block"

    wi = w_in[0].astype(BF16)
    o_q = 2 * DC
    o_z = o_q + QKV
    o_a = o_z + HV
    o_gate = o_a + 2 * HEADS
    row = lambda v: v.reshape(1, -1).astype(F32)
    lane_pad = lambda v: jnp.pad(row(v), ((0, 0), (0, LANE - v.shape[-1])))
    wu = w_up[0]
    wts = {
        "nmw": row(norm_mix_w[0]),
        "w_c": wi[:, :o_q],
        "w_qkv": wi[:, o_q:o_z],
        "w_z": wi[:, o_z:o_a],
        "w_gb": jnp.pad(wi[:, o_a:o_gate], ((0, 0), (0, LANE - 2 * HEADS))),
        "w_gate": wi[:, o_gate:],
        "b_gate": row(b_gate[0]),
        "alog": lane_pad(dn_A_log[0]),
        "dtb": lane_pad(dn_dt_bias[0]),
        "conf_w": conf_dw_w[0].astype(F32),
        "conf_b": row(conf_dw_b[0]),
        "ln_w": row(conf_ln_w[0]),
        "ln_b": row(conf_ln_b[0]),
        "dn_cw": dn_conv_w[0].astype(F32),
        "dn_nw": row(dn_norm_w[0]),
        "tri": (jnp.arange(CHUNK)[:, None] >= jnp.arange(CHUNK)[None, :]).astype(BF16),
        "masks": _level_masks(),
        "w_co": w_conf_out[0].astype(BF16),
        "w_do": w_dn_out[0].astype(BF16),
        "w_out": w_out[0].astype(BF16),
        "nfw": row(norm_ffn_w[0]),
        "w_upg": wu[:, :D_FF].astype(BF16),
        "w_upv": wu[:, D_FF:].astype(BF16),
        "dw_g": ffn_dw_w[0][:, :D_FF].astype(F32),
        "dw_v": ffn_dw_w[0][:, D_FF:].astype(F32),
        "db_g": row(ffn_dw_b[0][:D_FF]),
        "db_v": row(ffn_dw_b[0][D_FF:]),
        "w_down": w_down[0].astype(BF16),
        "nlw": row(norm_final_w),
    }

    zero_carries = (jnp.zeros((CONF_HALO, DC), F32), jnp.zeros((QKV_HALO, QKV), F32),
                    jnp.zeros(STATE_SHAPE, F32), jnp.zeros((FFN_HALO, D_FF), F32),
                    jnp.zeros((FFN_HALO, D_FF), F32))
    head = jnp.concatenate([jnp.zeros((PAD, D_MODEL), x.dtype), meta_tokens.astype(x.dtype)])[None]
    _, carries = _layer(head, zero_carries, wts, tl=CHUNK, head_pass=True)
    out, _ = _layer(x, carries, wts, tl=BODY_ROWS, head_pass=False)
    return out
```
